```python
import math
import jax
import jax.numpy as jnp
from jax import lax
import numpy as np

D_MODEL = 4096
BATCH = 4
SEQ = 2048
DEPTH = 2
DEC_BATCH = 8
DEC_SEQ = 1
PAST_LEN = 16384
PAGE_SIZE = 128

D_MIX = D_MODEL
D_HGRN = D_MIX // 2
HGRN_DK = 128
HGRN_HEADS = D_HGRN // HGRN_DK
HGRN_DV = D_HGRN // HGRN_HEADS
HGRN_CHUNK = 32
D_NSA = D_MIX - D_HGRN
NSA_HD = 128
NSA_HEADS = D_NSA // NSA_HD
NSA_KV_HEADS = max(1, NSA_HEADS // 4)
NSA_GROUP = NSA_HEADS // NSA_KV_HEADS
BLOCK = 64
N_SELECT = 16
WINDOW = 512
SEL_QBLOCK = 16
WIN_QBLOCK = 128
KV_W = NSA_KV_HEADS * 2 * NSA_HD
N_IN = 4 * D_HGRN + D_NSA + 3 * KV_W + 3 * NSA_HEADS
D_FF = 4 * D_MODEL
EPS = 1e-6
NEG_INF = -1e30
LB_FLOOR = 1e-30
ATTN_SCALE = NSA_HD ** -0.5
FORCE_SCORE = NSA_GROUP + 1.0

kernel_name = 'hymba_hgrn2_nsa_decode_step'


def rms_norm(x, g):
    xf = x.astype(jnp.float32)
    y = xf * lax.rsqrt(jnp.mean(xf * xf, axis=-1, keepdims=True) + EPS)
    return (y * g.astype(jnp.float32)).astype(x.dtype)


def masked_softmax(s, mask):
    s = jnp.where(mask, s, NEG_INF)
    e = jnp.where(mask, jnp.exp(s - jnp.max(s, axis=-1, keepdims=True)), 0.0)
    return e / jnp.maximum(jnp.sum(e, axis=-1, keepdims=True), 1e-30)


def lower_bounds(lb_logits):
    p = jax.nn.softmax(lb_logits.astype(jnp.float32), axis=0)
    return jnp.cumsum(p, axis=0) - p[0:1]


def split_proj(u):
    sizes = (D_HGRN, D_HGRN, D_HGRN, D_HGRN, D_NSA, KV_W, KV_W, KV_W, 3 * NSA_HEADS)
    return jnp.split(u, np.cumsum(sizes)[:-1].tolist(), axis=-1)


def hgrn2(hq, hf, hi, hg, lb, gn, s0):
    f32 = jnp.float32
    b, t = hq.shape[:2]
    q = jax.nn.silu(hq.astype(f32)).reshape(b, t, HGRN_HEADS, HGRN_DK)
    z = hf.astype(f32).reshape(b, t, HGRN_HEADS, HGRN_DK)
    lb = lb.astype(f32).reshape(HGRN_HEADS, HGRN_DK)
    logf = jnp.logaddexp(jnp.log(jnp.maximum(lb, LB_FLOOR)), jnp.log1p(-lb) + jax.nn.log_sigmoid(z))
    k = (1.0 - lb) * jax.nn.sigmoid(-z)
    v = hi.astype(f32).reshape(b, t, HGRN_HEADS, HGRN_DV)
    c = math.gcd(t, HGRN_CHUNK)
    nc = t // c

    def chunks(a):
        return a.reshape(b, nc, c, HGRN_HEADS, a.shape[-1]).transpose(1, 0, 3, 2, 4)

    tril = jnp.tril(jnp.ones((c, c), bool))[:, :, None]

    def step(S, xs):
        qc, kc, vc, lc = xs
        cb = jnp.cumsum(lc, axis=2)
        dec = jnp.exp(jnp.where(tril, cb[:, :, :, None, :] - cb[:, :, None, :, :], NEG_INF))
        att = jnp.einsum('bhtd,bhsd,bhtsd->bhts', qc, kc, dec)
        o = jnp.einsum('bhts,bhsv->bhtv', att, vc) + jnp.einsum('bhtd,bhdv->bhtv', qc * jnp.exp(cb), S)
        last = cb[:, :, -1]
        S = jnp.exp(last)[..., None] * S + jnp.einsum('bhsd,bhsv->bhdv', kc * jnp.exp(last[:, :, None] - cb), vc)
        return S, o

    S, o = lax.scan(step, s0.astype(f32), (chunks(q), chunks(k), chunks(v), chunks(logf)))
    o = o.transpose(1, 0, 3, 2, 4).reshape(b, t, HGRN_HEADS, HGRN_DV)
    o = rms_norm(o, gn.reshape(HGRN_HEADS, HGRN_DV)) * jax.nn.silu(hg.astype(f32)).reshape(b, t, HGRN_HEADS, HGRN_DV)
    return o.reshape(b, t, D_HGRN).astype(hq.dtype), S


def project_and_recur(h, w_in_l, lb_l, gn_l, s0):
    b, t = h.shape[:2]
    hq, hf, hi, hg, nq, ckv, skv, wkv, gate = split_proj(h @ w_in_l)
    o_r, s_new = hgrn2(hq, hf, hi, hg, lb_l, gn_l, s0)
    kvs = lambda a: a.reshape(b, t, NSA_KV_HEADS, 2, NSA_HD)
    q = nq.reshape(b, t, NSA_KV_HEADS, NSA_GROUP, NSA_HD)
    return o_r, s_new, q, kvs(ckv), kvs(skv), kvs(wkv), gate


def compress(kv, pe, wc):
    b, l = kv.shape[:2]
    nb = l // BLOCK
    blk = kv[:, :nb * BLOCK].reshape(b, nb, BLOCK, NSA_KV_HEADS, 2, NSA_HD) + pe[:, None].astype(kv.dtype)
    return jnp.einsum('bnrhcd,crdo->bnhco', blk, wc)


def local_gather(rows_flat, b, seq_len):
    bi = jnp.arange(b)[:, None, None, None]
    hi = jnp.arange(NSA_KV_HEADS)[None, None, :, None, None]
    r = jnp.arange(BLOCK)

    def gather(idx):
        rows = (bi * seq_len + idx * BLOCK)[..., None] + r
        return rows_flat[rows, hi]
    return gather


def paged_gather(pool_rows, new_rows, page_table, t_pad):
    db, n_pages = page_table.shape
    nbp = n_pages * PAGE_SIZE // BLOCK
    bi = jnp.arange(db)[:, None, None, None]
    hi = jnp.arange(NSA_KV_HEADS)[None, None, :, None, None]
    r = jnp.arange(BLOCK)

    def gather(idx):
        ppos = jnp.minimum(idx, nbp - 1) * BLOCK
        phys = page_table[bi, ppos // PAGE_SIZE]
        row_p = (phys * PAGE_SIZE + ppos % PAGE_SIZE)[..., None] + r
        row_n = (bi * t_pad + jnp.clip(idx - nbp, 0, t_pad // BLOCK - 1) * BLOCK)[..., None] + r
        past = pool_rows[row_p, hi]
        new = new_rows[row_n, hi].astype(past.dtype)
        return jnp.where((idx < nbp)[..., None, None, None], past, new)
    return gather


def cmp_sel_attend(q, pos0, kc, n_blk, gather):
    f32 = jnp.float32
    b, t = q.shape[:2]
    qb = math.gcd(t, SEL_QBLOCK)
    nq = t // qb
    nbc = kc.shape[1]
    n_sel = min(N_SELECT, n_blk)
    blk_end = jnp.arange(nbc) * BLOCK + (BLOCK - 1)
    jj = jnp.arange(n_blk)[None, None, None, :]
    r = jnp.arange(BLOCK)

    def one(args):
        i, qi = args
        qpos = pos0 + i * qb + jnp.arange(qb)
        s = jnp.einsum('bthgd,bnhd->bthgn', qi, kc[..., 0, :], preferred_element_type=f32) * ATTN_SCALE
        p = masked_softmax(s, (blk_end[None, :] <= qpos[:, None])[None, :, None, None, :])
        o_c = jnp.einsum('bthgn,bnhd->bthgd', p.astype(qi.dtype), kc[..., 1, :])
        imp = jnp.pad(jnp.sum(p, axis=3), ((0, 0), (0, 0), (0, 0), (0, n_blk - nbc)))
        cur = (qpos // BLOCK)[None, :, None, None]
        forced = (jj == 0) | (jj == cur) | (jj == cur - 1)
        score = jnp.where(jj <= cur, jnp.where(forced, FORCE_SCORE, imp), -1.0)
        _, idx = lax.top_k(score, n_sel)
        valid = idx <= cur
        kv = gather(idx)
        kpos = idx[..., None] * BLOCK + r
        mask = valid[..., None] & (kpos <= qpos[None, :, None, None, None])
        s = jnp.einsum('bthgd,bthnrd->bthgnr', qi, kv[..., 0, :], preferred_element_type=f32) * ATTN_SCALE
        s = s.reshape(b, qb, NSA_KV_HEADS, NSA_GROUP, n_sel * BLOCK)
        p = masked_softmax(s, mask.reshape(b, qb, NSA_KV_HEADS, 1, n_sel * BLOCK))
        vs = kv[..., 1, :].reshape(b, qb, NSA_KV_HEADS, n_sel * BLOCK, NSA_HD)
        o_s = jnp.einsum('bthgk,bthkd->bthgd', p.astype(qi.dtype), vs.astype(qi.dtype))
        return o_c, o_s

    qs = q.reshape(b, nq, qb, NSA_KV_HEADS, NSA_GROUP, NSA_HD).swapaxes(0, 1)
    o_c, o_s = lax.map(one, (jnp.arange(nq), qs))
    unblock = lambda o: o.swapaxes(0, 1).reshape(b, t, NSA_KV_HEADS, NSA_GROUP, NSA_HD)
    return unblock(o_c), unblock(o_s)


def window_attend(q, pos0, kv_ext):
    f32 = jnp.float32
    b, t = q.shape[:2]
    qb = math.gcd(t, WIN_QBLOCK)
    nq = t // qb

    def one(args):
        i, qi = args
        kv = lax.dynamic_slice_in_dim(kv_ext, i * qb, WINDOW + qb, axis=1)
        qpos = pos0 + i * qb + jnp.arange(qb)
        kpos = pos0 - WINDOW + i * qb + jnp.arange(WINDOW + qb)
        d = qpos[:, None] - kpos[None, :]
        mask = (d >= 0) & (d < WINDOW) & (kpos[None, :] >= 0)
        s = jnp.einsum('bthgd,bkhd->bthgk', qi, kv[..., 0, :], preferred_element_type=f32) * ATTN_SCALE
        p = masked_softmax(s, mask[None, :, None, None, :])
        return jnp.einsum('bthgk,bkhd->bthgd', p.astype(qi.dtype), kv[..., 1, :].astype(qi.dtype))

    qs = q.reshape(b, nq, qb, NSA_KV_HEADS, NSA_GROUP, NSA_HD).swapaxes(0, 1)
    o = lax.map(one, (jnp.arange(nq), qs))
    return o.swapaxes(0, 1).reshape(b, t, NSA_KV_HEADS, NSA_GROUP, NSA_HD)


def combine(gate, o_c, o_s, o_w):
    b, t = gate.shape[:2]
    g = jax.nn.sigmoid(gate.astype(jnp.float32)).astype(o_c.dtype).reshape(b, t, 3, NSA_KV_HEADS, NSA_GROUP, 1)
    o = g[:, :, 0] * o_c + g[:, :, 1] * o_s + g[:, :, 2] * o_w
    return o.reshape(b, t, D_NSA)


def finish(x, o_r, o_b, w_out_l, w_up_l, w_down_l, g_post_mix, g_pre_ffn, g_post_ffn):
    mix = jnp.concatenate([o_r, o_b.astype(o_r.dtype)], axis=-1)
    h = x + rms_norm(mix @ w_out_l, g_post_mix)
    f = rms_norm(h, g_pre_ffn)
    f = jnp.square(jax.nn.relu(f @ w_up_l)) @ w_down_l
    return h + rms_norm(f, g_post_ffn)


def setup_inputs(seed: int = 0) -> dict:
    key = jax.random.key(seed)
    ks = jax.random.split(key, 20)
    f32 = jnp.float32

    def nrm(k, shape, scale=1.0):
        return scale * jax.random.normal(k, shape, f32)

    n_pages = PAST_LEN // PAGE_SIZE
    n_used = DEC_BATCH * n_pages
    n_pool = n_used + max(1, n_used // 4)
    win_buf = min(WINDOW, PAST_LEN)
    kv_pool = (DEPTH, n_pool, PAGE_SIZE, NSA_KV_HEADS, 2, NSA_HD)
    page_table = jax.random.permutation(ks[6], n_pool)[:n_used].reshape(DEC_BATCH, n_pages).astype(jnp.int32)
    return {
        'x_prompt': nrm(ks[0], (BATCH, SEQ, D_MODEL)),
        'x_sample': nrm(ks[1], (DEC_BATCH, DEC_SEQ, D_MODEL)),
        'cache_cmp_kv': nrm(ks[2], kv_pool),
        'cache_sel_kv': nrm(ks[3], kv_pool),
        'state_win_kv': nrm(ks[4], (DEPTH, DEC_BATCH, win_buf, NSA_KV_HEADS, 2, NSA_HD)),
        'state_hgrn': nrm(ks[5], (DEPTH, DEC_BATCH, HGRN_HEADS, HGRN_DK, HGRN_DV), 0.5),
        'page_table': page_table,
        'w_in': nrm(ks[7], (DEPTH, D_MODEL, N_IN), D_MODEL ** -0.5),
        'w_out': nrm(ks[8], (DEPTH, D_MIX, D_MODEL), D_MIX ** -0.5),
        'w_up': nrm(ks[9], (DEPTH, D_MODEL, D_FF), D_MODEL ** -0.5),
        'w_down': nrm(ks[10], (DEPTH, D_FF, D_MODEL), D_FF ** -0.5),
        'cmp_pe': nrm(ks[11], (DEPTH, BLOCK, 2, NSA_HD), 0.1),
        'cmp_w': nrm(ks[12], (DEPTH, 2, BLOCK, NSA_HD, NSA_HD), (BLOCK * NSA_HD) ** -0.5),
        'lb_logits': nrm(ks[13], (DEPTH, D_HGRN), 0.1),
        'hgrn_norm': 1.0 + nrm(ks[14], (DEPTH, D_HGRN), 0.02),
        'norm_pre_mix': 1.0 + nrm(ks[15], (DEPTH, D_MODEL), 0.02),
        'norm_post_mix': 1.0 + nrm(ks[16], (DEPTH, D_MODEL), 0.02),
        'norm_pre_ffn': 1.0 + nrm(ks[17], (DEPTH, D_MODEL), 0.02),
        'norm_post_ffn': 1.0 + nrm(ks[18], (DEPTH, D_MODEL), 0.02),
    }


def reference(x_prompt, x_sample, cache_cmp_kv, cache_sel_kv, state_win_kv, state_hgrn, page_table,
              w_in, w_out, w_up, w_down, cmp_pe, cmp_w, lb_logits, hgrn_norm,
              norm_pre_mix, norm_post_mix, norm_pre_ffn, norm_post_ffn):
    lbs = lower_bounds(lb_logits)
    past_len = page_table.shape[1] * PAGE_SIZE
    xp, xs = x_prompt, x_sample
    bp, tp = xp.shape[:2]
    bs, ts = xs.shape[:2]
    t_pad = -(-ts // BLOCK) * BLOCK
    kv_pad = lambda a, front, back: jnp.pad(a, ((0, 0), (front, back), (0, 0), (0, 0), (0, 0)))
    p_cmp, p_sel, p_win, p_rec = [], [], [], []
    s_cmp, s_sel, s_win, s_rec = [], [], [], []
    for l in range(DEPTH):
        h = rms_norm(xp, norm_pre_mix[l])
        s0 = jnp.zeros((bp, HGRN_HEADS, HGRN_DK, HGRN_DV), jnp.float32)
        o_r, st, q, ckv, skv, wkv, gate = project_and_recur(h, w_in[l], lbs[l], hgrn_norm[l], s0)
        kc = compress(ckv, cmp_pe[l], cmp_w[l])
        gather = local_gather(skv.reshape(bp * tp, NSA_KV_HEADS, 2, NSA_HD), bp, tp)
        o_c, o_s = cmp_sel_attend(q, 0, kc, tp // BLOCK, gather)
        o_w = window_attend(q, 0, kv_pad(wkv, WINDOW, 0))
        xp = finish(xp, o_r, combine(gate, o_c, o_s, o_w), w_out[l], w_up[l], w_down[l],
                    norm_post_mix[l], norm_pre_ffn[l], norm_post_ffn[l])
        p_cmp.append(ckv)
        p_sel.append(skv)
        p_win.append(wkv[:, tp - min(WINDOW, tp):])
        p_rec.append(st.astype(state_hgrn.dtype))

        h = rms_norm(xs, norm_pre_mix[l])
        o_r, st, q, ckv, skv, wkv, gate = project_and_recur(h, w_in[l], lbs[l], hgrn_norm[l], state_hgrn[l])
        past_c = cache_cmp_kv[l][page_table].reshape(bs, past_len, NSA_KV_HEADS, 2, NSA_HD)
        kc = jnp.concatenate([compress(past_c, cmp_pe[l], cmp_w[l]),
                              compress(ckv.astype(past_c.dtype), cmp_pe[l], cmp_w[l])], axis=1)
        new_rows = kv_pad(skv, 0, t_pad - ts).reshape(bs * t_pad, NSA_KV_HEADS, 2, NSA_HD)
        gather = paged_gather(cache_sel_kv[l].reshape(-1, NSA_KV_HEADS, 2, NSA_HD), new_rows, page_table, t_pad)
        o_c, o_s = cmp_sel_attend(q, past_len, kc, (past_len + t_pad) // BLOCK, gather)
        buf = state_win_kv[l]
        wb = buf.shape[1]
        kv_ext = jnp.concatenate([kv_pad(buf, WINDOW - wb, 0), wkv.astype(buf.dtype)], axis=1)
        o_w = window_attend(q, past_len, kv_ext)
        xs = finish(xs, o_r, combine(gate, o_c, o_s, o_w), w_out[l], w_up[l], w_down[l],
                    norm_post_mix[l], norm_pre_ffn[l], norm_post_ffn[l])
        s_cmp.append(ckv)
        s_sel.append(skv)
        s_win.append(kv_ext[:, kv_ext.shape[1] - wb:])
        s_rec.append(st.astype(state_hgrn.dtype))

    p_cmp_kv, p_sel_kv, p_win_kv, p_hgrn = jnp.stack(p_cmp), jnp.stack(p_sel), jnp.stack(p_win), jnp.stack(p_rec)
    s_cmp_kv, s_sel_kv, s_win_kv, s_hgrn = jnp.stack(s_cmp), jnp.stack(s_sel), jnp.stack(s_win), jnp.stack(s_rec)
    return (xp, xs, p_cmp_kv, p_sel_kv, p_win_kv, p_hgrn, s_cmp_kv, s_sel_kv, s_win_kv, s_hgrn)
```

```python
import functools
import math

import jax
import jax.numpy as jnp
from jax import lax
from jax.experimental import pallas as pl
from jax.experimental.pallas import tpu as pltpu

F32 = jnp.float32
BF16 = jnp.bfloat16

HGRN_DK = 128
NSA_HD = 128
NSA_GROUP = 4
BLOCK = 64
N_SELECT = 16
WINDOW = 512
PAGE_SIZE = 128
EPS = 1e-6
NEG_INF = -1e30
LB_FLOOR = 1e-30
ATTN_SCALE = NSA_HD ** -0.5
FORCE_SCORE = NSA_GROUP + 1.0

LANES = 128
VMEM_LIMIT = 56 * 1024 * 1024

HGRN_CHUNK = 64
HGRN_SUB = 16
PAGES_PER_STEP = 16


def _cparams(sem):
    return pltpu.CompilerParams(dimension_semantics=sem, vmem_limit_bytes=VMEM_LIMIT)


def _dot(a, b):
    return jnp.dot(a, b, preferred_element_type=F32)


def _dot_nt(a, b):
    return lax.dot_general(a, b, (((1,), (1,)), ((), ())), preferred_element_type=F32)


def _dot_tn(a, b):
    return lax.dot_general(a, b, (((0,), (0,)), ((), ())), preferred_element_type=F32)


def _sigmoid(x):
    return 1.0 / (1.0 + jnp.exp(-x))


def _silu(x):
    return x * _sigmoid(x)


def _masked_softmax(s, mask):
    s = jnp.where(mask, s, NEG_INF)
    e = jnp.where(mask, jnp.exp(s - jnp.max(s, axis=-1, keepdims=True)), 0.0)
    return e / jnp.maximum(jnp.sum(e, axis=-1, keepdims=True), 1e-30)


def _norm_matmul_body(x_ref, g_ref, w_ref, o_ref, h_ref, *, act):
    @pl.when(pl.program_id(1) == 0)
    def _():
        x = x_ref[...]
        ms = jnp.mean(x * x, axis=-1, keepdims=True)
        h_ref[...] = (x * lax.rsqrt(ms + EPS) * g_ref[...]).astype(BF16)

    acc = _dot(h_ref[...], w_ref[...])
    if act:
        acc = jnp.square(jnp.maximum(acc, 0.0))
    o_ref[...] = acc.astype(o_ref.dtype)


def _norm_matmul(x, g, w, *, act, out_dtype):
    m, k = x.shape
    n = w.shape[1]
    tm = min(m, 512)
    tn = next(t for t in (512, 384, 256, 128) if n % t == 0)
    return pl.pallas_call(
        functools.partial(_norm_matmul_body, act=act),
        grid=(m // tm, n // tn),
        in_specs=[
            pl.BlockSpec((tm, k), lambda i, j: (i, 0)),
            pl.BlockSpec((1, k), lambda i, j: (0, 0)),
            pl.BlockSpec((k, tn), lambda i, j: (0, j)),
        ],
        out_specs=pl.BlockSpec((tm, tn), lambda i, j: (i, j)),
        out_shape=jax.ShapeDtypeStruct((m, n), out_dtype),
        scratch_shapes=[pltpu.VMEM((tm, k), BF16)],
        compiler_params=_cparams(("parallel", "arbitrary")),
        name="norm_matmul",
    )(x, g.reshape(1, k), w)


def _matmul_norm_res_body(a_ref, w_ref, g_ref, r_ref, o_ref, *, nk):
    kk = pl.program_id(1)
    part = _dot(a_ref[...], w_ref[...])

    @pl.when(kk == 0)
    def _():
        o_ref[...] = part

    @pl.when(kk > 0)
    def _():
        o_ref[...] += part

    @pl.when(kk == nk - 1)
    def _():
        y = o_ref[...]
        ms = jnp.mean(y * y, axis=-1, keepdims=True)
        o_ref[...] = r_ref[...] + y * lax.rsqrt(ms + EPS) * g_ref[...]


def _matmul_norm_res(a, w, g, res, *, tk=512):
    m, k = a.shape
    n = w.shape[1]
    tm = min(m, 512)
    nk = k // tk
    return pl.pallas_call(
        functools.partial(_matmul_norm_res_body, nk=nk),
        grid=(m // tm, nk),
        in_specs=[
            pl.BlockSpec((tm, tk), lambda i, j: (i, j)),
            pl.BlockSpec((tk, n), lambda i, j: (j, 0)),
            pl.BlockSpec((1, n), lambda i, j: (0, 0)),
            pl.BlockSpec((tm, n), lambda i, j: (i, 0), pipeline_mode=pl.Buffered(1)),
        ],
        out_specs=pl.BlockSpec((tm, n), lambda i, j: (i, 0), pipeline_mode=pl.Buffered(1)),
        out_shape=jax.ShapeDtypeStruct((m, n), F32),
        compiler_params=_cparams(("parallel", "arbitrary")),
        name="matmul_norm_res",
    )(a, w, g.reshape(1, n), res)


def _hgrn_gates(z, lbp):
    a = lbp[0:1, :]
    log_sig = jnp.minimum(z, 0.0) - jnp.log1p(jnp.exp(-jnp.abs(z)))
    b = lbp[1:2, :] + log_sig
    logf = jnp.maximum(a, b) + jnp.log1p(jnp.exp(-jnp.abs(a - b)))
    k = lbp[2:3, :] * _sigmoid(-z)
    return logf, k


def _hgrn_out(o, gn, hg):
    ms = jnp.mean(o * o, axis=-1, keepdims=True)
    return o * lax.rsqrt(ms + EPS) * gn * _silu(hg)


def _hgrn_prompt_body(q_ref, f_ref, i_ref, g_ref, lb_ref, gn_ref, o_ref, st_ref, s_ref, *, n_chunks):
    c, sub = HGRN_CHUNK, HGRN_SUB
    ns = c // sub
    tstep = pl.program_id(2)

    @pl.when(tstep == 0)
    def _():
        s_ref[...] = jnp.zeros_like(s_ref)

    lbp = lb_ref[...]
    gn = gn_ref[...]
    row = lax.broadcasted_iota(jnp.int32, (c, c), 0)
    col = lax.broadcasted_iota(jnp.int32, (c, c), 1)
    tri = jnp.where(row >= col, 1.0, 0.0).astype(BF16)
    ones = jnp.ones((LANES, LANES), BF16)
    t_in_sub = lax.broadcasted_iota(jnp.int32, (ns, sub, HGRN_DK), 1)

    def chunk(ci, carry):
        r0 = pl.multiple_of(ci * c, c)
        rows = pl.ds(r0, c)
        z = f_ref[rows, :]
        q = _silu(q_ref[rows, :])
        v = i_ref[rows, :]
        logf, k = _hgrn_gates(z, lbp)

        hi = logf.astype(BF16)
        r1 = logf - hi.astype(F32)
        mid = r1.astype(BF16)
        lo = (r1 - mid.astype(F32)).astype(BF16)
        cb = _dot(tri, hi) + _dot(tri, mid) + _dot(tri, lo)

        q3 = q.reshape(ns, sub, HGRN_DK)
        k3 = k.reshape(ns, sub, HGRN_DK)
        v3 = v.reshape(ns, sub, HGRN_DK)
        cb3 = cb.reshape(ns, sub, HGRN_DK)
        e_end = cb3[:, sub - 1:sub, :]
        b_beg = jnp.concatenate([jnp.zeros((1, 1, HGRN_DK), F32), e_end[:ns - 1]], axis=0)
        qt = q3 * jnp.exp(cb3 - b_beg)
        kt = k3 * jnp.exp(e_end - cb3)

        q_cols, k_cols = [], []
        zeros_sub = jnp.zeros((sub, HGRN_DK), F32)
        for j in range(ns - 1):
            qc, kc = [], []
            for i in range(ns):
                if i > j:
                    qc.append(qt[i] * jnp.exp(b_beg[i] - e_end[j]))
                else:
                    qc.append(zeros_sub)
                kc.append(kt[i] if i == j else zeros_sub)
            q_cols.append(jnp.concatenate(qc, axis=0))
            k_cols.append(jnp.concatenate(kc, axis=0))
        q_hat = jnp.concatenate(q_cols, axis=1).astype(BF16)
        k_hat = jnp.concatenate(k_cols, axis=1).astype(BF16)
        att_off = _dot_nt(q_hat, k_hat)
        o = _dot(att_off.astype(BF16), v.astype(BF16))

        parts = []
        for s in range(sub):
            dec = jnp.exp(cb3 - cb3[:, s:s + 1, :])
            p = jnp.where(t_in_sub >= s, q3 * k3[:, s:s + 1, :] * dec, 0.0)
            parts.append(p.reshape(c, HGRN_DK).astype(BF16))
        rsum = _dot(jnp.concatenate(parts, axis=0), ones)
        o_diag = jnp.zeros((ns, sub, HGRN_DK), F32)
        for s in range(sub):
            o_diag = o_diag + rsum[s * c:(s + 1) * c, :].reshape(ns, sub, HGRN_DK) * v3[:, s:s + 1, :]
        o = o + o_diag.reshape(c, HGRN_DK)

        st = s_ref[...]
        o = o + _dot_nt((q * jnp.exp(cb)).astype(BF16), st.astype(BF16))
        last = cb[c - 1:c, :]
        kd = (k * jnp.exp(last - cb)).astype(BF16)
        s_ref[...] = st * jnp.exp(last) + _dot_tn(v.astype(BF16), kd)

        o_ref[rows, :] = _hgrn_out(o, gn, g_ref[rows, :]).astype(o_ref.dtype)
        return carry

    lax.fori_loop(0, n_chunks, chunk, 0)

    @pl.when(tstep == pl.num_programs(2) - 1)
    def _():
        st_ref[...] = s_ref[...].T


def _hgrn_prompt(u, lbp, gn, *, batch, seq, heads, tt):
    nt = seq // tt
    blk = lambda off: pl.BlockSpec((tt, HGRN_DK), lambda b, h, t: (b * nt + t, off + h))
    return pl.pallas_call(
        functools.partial(_hgrn_prompt_body, n_chunks=tt // HGRN_CHUNK),
        grid=(batch, heads, nt),
        in_specs=[
            blk(0), blk(heads), blk(2 * heads), blk(3 * heads),
            pl.BlockSpec((None, 8, HGRN_DK), lambda b, h, t: (h, 0, 0)),
            pl.BlockSpec((None, 1, HGRN_DK), lambda b, h, t: (h, 0, 0)),
        ],
        out_specs=[
            pl.BlockSpec((tt, HGRN_DK), lambda b, h, t: (b * nt + t, h)),
            pl.BlockSpec((None, None, HGRN_DK, HGRN_DK), lambda b, h, t: (b, h, 0, 0)),
        ],
        out_shape=[
            jax.ShapeDtypeStruct((batch * seq, heads * HGRN_DK), BF16),
            jax.ShapeDtypeStruct((batch, heads, HGRN_DK, HGRN_DK), F32),
        ],
        scratch_shapes=[pltpu.VMEM((HGRN_DK, HGRN_DK), F32)],
        compiler_params=_cparams(("parallel", "parallel", "arbitrary")),
        name="hgrn_prompt",
    )(u, u, u, u, lbp, gn)


def _hgrn_decode_body(q_ref, f_ref, i_ref, g_ref, lb_ref, gn_ref, s0_ref, o_ref, st_ref):
    r = pl.ds(pl.program_id(1), 1)
    logf, k = _hgrn_gates(f_ref[r, :], lb_ref[...])
    q = _silu(q_ref[r, :])
    v = i_ref[r, :]

    def col(x):
        return jnp.broadcast_to(x, (HGRN_DK, HGRN_DK)).T

    s_new = col(jnp.exp(logf)) * s0_ref[...] + col(k) * v
    st_ref[...] = s_new
    o = jnp.sum(col(q) * s_new, axis=0, keepdims=True)
    o_ref[r, :] = _hgrn_out(o, gn_ref[...], g_ref[r, :])


def _hgrn_decode(u, lbp, gn, s0, *, batch, heads):
    blk = lambda off: pl.BlockSpec((batch, HGRN_DK), lambda h, b: (0, off + h))
    return pl.pallas_call(
        _hgrn_decode_body,
        grid=(heads, batch),
        in_specs=[
            blk(0), blk(heads), blk(2 * heads), blk(3 * heads),
            pl.BlockSpec((None, 8, HGRN_DK), lambda h, b: (h, 0, 0)),
            pl.BlockSpec((None, 1, HGRN_DK), lambda h, b: (h, 0, 0)),
            pl.BlockSpec((None, None, HGRN_DK, HGRN_DK), lambda h, b: (b, h, 0, 0)),
        ],
        out_specs=[
            pl.BlockSpec((batch, HGRN_DK), lambda h, b: (0, h)),
            pl.BlockSpec((None, None, HGRN_DK, HGRN_DK), lambda h, b: (b, h, 0, 0)),
        ],
        out_shape=[
            jax.ShapeDtypeStruct((batch, heads * HGRN_DK), F32),
            jax.ShapeDtypeStruct(s0.shape, F32),
        ],
        compiler_params=_cparams(("parallel", "arbitrary")),
        name="hgrn_decode",
    )(u, u, u, u, lbp, gn, s0)


def _compress_body(x_ref, pe_ref, w_ref, o_ref):
    o_ref[...] = _dot((x_ref[...] + pe_ref[...]).astype(BF16), w_ref[...])


def _compress(x, pe, w):
    _, r, kdim = x.shape
    tm = min(r, 256)
    return pl.pallas_call(
        _compress_body,
        grid=(2, r // tm),
        in_specs=[
            pl.BlockSpec((None, tm, kdim), lambda c, i: (c, i, 0)),
            pl.BlockSpec((None, 1, kdim), lambda c, i: (c, 0, 0)),
            pl.BlockSpec((None, kdim, NSA_HD), lambda c, i: (c, 0, 0)),
        ],
        out_specs=pl.BlockSpec((None, tm, NSA_HD), lambda c, i: (c, i, 0)),
        out_shape=jax.ShapeDtypeStruct((2, r, NSA_HD), F32),
        compiler_params=_cparams(("parallel", "parallel")),
        name="compress",
    )(x, pe, w)


def _compress_paged_body(pt_ref, *refs, kvh):
    del pt_ref
    npg = PAGES_PER_STEP
    page_refs = refs[:npg]
    pe_ref, w_ref, o_ref, buf_ref = refs[npg:]
    for p in range(npg):
        for hc in range(2 * kvh):
            buf_ref[hc, p * PAGE_SIZE:(p + 1) * PAGE_SIZE, :] = page_refs[p][:, hc * NSA_HD:(hc + 1) * NSA_HD]
    nblk = npg * PAGE_SIZE // BLOCK

    def body(r, accs):
        out = []
        for c in range(2):
            pieces = [buf_ref[2 * h + c, pl.ds(r, nblk, stride=BLOCK), :] for h in range(kvh)]
            x = jnp.concatenate(pieces, axis=0) + pe_ref[c, pl.ds(r, 1), :]
            out.append(accs[c] + _dot(x.astype(BF16), w_ref[c, r]))
        return tuple(out)

    zero = jnp.zeros((kvh * nblk, NSA_HD), F32)
    accs = lax.fori_loop(0, BLOCK, body, (zero, zero))
    for c in range(2):
        o_ref[c] = accs[c].reshape(kvh, nblk, NSA_HD)


def _compress_paged(pool, page_table, pe, w, *, layer, kvh):
    batch, n_pages = page_table.shape
    npg = PAGES_PER_STEP
    width = pool.shape[-1]
    nblk = npg * PAGE_SIZE // BLOCK

    def page_spec(p):
        return pl.BlockSpec((None, None, PAGE_SIZE, width),
                            lambda b, g, pt: (layer, pt[b, g * npg + p], 0, 0))

    grid_spec = pltpu.PrefetchScalarGridSpec(
        num_scalar_prefetch=1,
        grid=(batch, n_pages // npg),
        in_specs=[page_spec(p) for p in range(npg)] + [
            pl.BlockSpec((2, BLOCK, NSA_HD), lambda b, g, pt: (0, 0, 0)),
            pl.BlockSpec((2, BLOCK, NSA_HD, NSA_HD), lambda b, g, pt: (0, 0, 0, 0)),
        ],
        out_specs=pl.BlockSpec((2, None, kvh, nblk, NSA_HD), lambda b, g, pt: (0, b, 0, g, 0)),
        scratch_shapes=[pltpu.VMEM((2 * kvh, npg * PAGE_SIZE, NSA_HD), F32)],
    )
    return pl.pallas_call(
        functools.partial(_compress_paged_body, kvh=kvh),
        grid_spec=grid_spec,
        out_shape=jax.ShapeDtypeStruct((2, batch, kvh, n_pages * PAGE_SIZE // BLOCK, NSA_HD), F32),
        compiler_params=_cparams(("parallel", "arbitrary")),
        name="compress_paged",
    )(page_table, *([pool] * npg), pe, w)


def _nsa_prompt_body(q_ref, kc_ref, skv_ref, wkv_ref, gate_ref, e_ref, o_ref,
                     ks_ref, vs_ref, kw_ref, vw_ref, *, tq, seq, n_blk, span):
    i = pl.program_id(2)

    @pl.when(i == 0)
    def _():
        ks_ref[...] = skv_ref[:, :NSA_HD].astype(BF16)
        vs_ref[...] = skv_ref[:, NSA_HD:].astype(BF16)
        kw_ref[...] = wkv_ref[:, :NSA_HD].astype(BF16)
        vw_ref[...] = wkv_ref[:, NSA_HD:].astype(BF16)

    nbp = kc_ref.shape[1]
    nb8 = -(-n_blk // 8) * 8
    n_sel = min(N_SELECT, n_blk)
    q0 = i * tq
    qpos = q0 + lax.broadcasted_iota(jnp.int32, (tq, 1), 0)
    cur = qpos // BLOCK
    n_lane = lax.broadcasted_iota(jnp.int32, (1, nbp), 1)
    cmask = (n_lane * BLOCK + (BLOCK - 1) <= qpos) & (n_lane < seq // BLOCK)

    kc = kc_ref[0]
    vc = kc_ref[1]
    qs = [q_ref[:, g * NSA_HD:(g + 1) * NSA_HD].astype(BF16) for g in range(NSA_GROUP)]

    o_c = []
    imp = jnp.zeros((tq, nbp), F32)
    for g in range(NSA_GROUP):
        p = _masked_softmax(_dot_nt(qs[g], kc) * ATTN_SCALE, cmask)
        o_c.append(_dot(p.astype(BF16), vc))
        imp = imp + p

    forced = (n_lane == 0) | (n_lane == cur) | (n_lane == cur - 1)
    score = jnp.where(n_lane <= cur, jnp.where(forced, FORCE_SCORE, imp), -1.0)
    score = jnp.where(n_lane < n_blk, score, -2.0)
    score_t = score.T[:nb8, :]
    j_idx = lax.broadcasted_iota(jnp.int32, (nb8, tq), 0)
    cnt = jnp.zeros((nb8, tq), F32)
    for j in range(n_blk):
        rj = score_t[j:j + 1, :]
        beats = jnp.where(rj > score_t, 1.0, jnp.where((rj == score_t) & (j_idx > j), 1.0, 0.0))
        cnt = cnt + beats
    cur_t = (q0 + lax.broadcasted_iota(jnp.int32, (1, tq), 1)) // BLOCK
    sel_t = jnp.where((cnt < n_sel) & (j_idx <= cur_t) & (j_idx < n_blk), 1.0, 0.0)
    if nbp > nb8:
        sel_t = jnp.concatenate([sel_t, jnp.zeros((nbp - nb8, tq), F32)], axis=0)
    sel = sel_t.T.astype(BF16)
    kpos = lax.broadcasted_iota(jnp.int32, (1, seq), 1)
    smask = (_dot(sel, e_ref[...]) > 0.5) & (kpos <= qpos)

    start = pl.multiple_of(jnp.maximum(q0 + tq - span, 0), tq)
    wpos = start + lax.broadcasted_iota(jnp.int32, (1, span), 1)
    dist = qpos - wpos
    wmask = (dist >= 0) & (dist < WINDOW)
    kw = kw_ref[pl.ds(start, span), :]
    vw = vw_ref[pl.ds(start, span), :]

    gates = _sigmoid(gate_ref[...])
    for g in range(NSA_GROUP):
        p = _masked_softmax(_dot_nt(qs[g], ks_ref[...]) * ATTN_SCALE, smask)
        o_s = _dot(p.astype(BF16), vs_ref[...])
        p = _masked_softmax(_dot_nt(qs[g], kw) * ATTN_SCALE, wmask)
        o_w = _dot(p.astype(BF16), vw)
        gc = gates[:, g:g + 1]
        gs = gates[:, NSA_GROUP + g:NSA_GROUP + g + 1]
        gw = gates[:, 2 * NSA_GROUP + g:2 * NSA_GROUP + g + 1]
        o = gc * o_c[g] + gs * o_s + gw * o_w
        o_ref[:, g * NSA_HD:(g + 1) * NSA_HD] = o.astype(o_ref.dtype)


def _nsa_prompt(u, kc, expand, *, batch, seq, kvh, col_q, col_skv, col_wkv, col_gate):
    tq = min(seq, 128)
    nq = seq // tq
    span = min(seq, WINDOW + tq)
    qw = NSA_GROUP * NSA_HD
    nbp = kc.shape[3]
    return pl.pallas_call(
        functools.partial(_nsa_prompt_body, tq=tq, seq=seq, n_blk=seq // BLOCK, span=span),
        grid=(batch, kvh, nq),
        in_specs=[
            pl.BlockSpec((tq, qw), lambda b, h, i: (b * nq + i, col_q // qw + h)),
            pl.BlockSpec((None, None, 2, nbp, NSA_HD), lambda b, h, i: (b, h, 0, 0, 0)),
            pl.BlockSpec((seq, 2 * NSA_HD), lambda b, h, i: (b, col_skv // (2 * NSA_HD) + h)),
            pl.BlockSpec((seq, 2 * NSA_HD), lambda b, h, i: (b, col_wkv // (2 * NSA_HD) + h)),
            pl.BlockSpec((tq, LANES), lambda b, h, i: (b * nq + i, col_gate // LANES + h)),
            pl.BlockSpec((nbp, seq), lambda b, h, i: (0, 0)),
        ],
        out_specs=pl.BlockSpec((tq, qw), lambda b, h, i: (b * nq + i, h)),
        out_shape=jax.ShapeDtypeStruct((batch * seq, kvh * qw), BF16),
        scratch_shapes=[pltpu.VMEM((seq, NSA_HD), BF16)] * 4,
        compiler_params=_cparams(("parallel", "parallel", "arbitrary")),
        name="nsa_prompt",
    )(u, kc, u, u, u, expand)


def _nsa_decode_select_body(q_ref, kc_ref, oc_ref, idx_ref, *, past_len, n_blk):
    nbc = kc_ref.shape[1]
    npad = -(-n_blk // LANES) * LANES
    n_sel = min(N_SELECT, n_blk)
    qpos = past_len
    cur = qpos // BLOCK
    q = q_ref[...].astype(BF16)
    n_lane = lax.broadcasted_iota(jnp.int32, (1, nbc), 1)
    cmask = n_lane * BLOCK + (BLOCK - 1) <= qpos
    p = _masked_softmax(_dot_nt(q, kc_ref[0].astype(BF16)) * ATTN_SCALE, cmask)
    oc_ref[...] = _dot(p.astype(BF16), kc_ref[1].astype(BF16))
    imp = jnp.sum(p, axis=0, keepdims=True)
    imp = jnp.concatenate([imp, jnp.zeros((1, npad - nbc), F32)], axis=1)

    j_lane = lax.broadcasted_iota(jnp.int32, (1, npad), 1)
    forced = (j_lane == 0) | (j_lane == cur) | (j_lane == cur - 1)
    score = jnp.where(j_lane <= cur, jnp.where(forced, FORCE_SCORE, imp), -1.0)
    score = jnp.where(j_lane < n_blk, score, -2.0)
    s_row = jnp.broadcast_to(score, (npad, npad))
    s_col = s_row.T
    jp = lax.broadcasted_iota(jnp.int32, (npad, npad), 0)
    jj = lax.broadcasted_iota(jnp.int32, (npad, npad), 1)
    beats = jnp.where(s_col > s_row, 1.0, jnp.where((s_col == s_row) & (jp < jj), 1.0, 0.0))
    rank = jnp.sum(beats, axis=0, keepdims=True)
    k_sub = lax.broadcasted_iota(jnp.int32, (N_SELECT, npad), 0).astype(F32)
    j_f = lax.broadcasted_iota(jnp.int32, (N_SELECT, npad), 1).astype(F32)
    hit = jnp.where((rank == k_sub) & (j_f < n_blk), j_f, 0.0)
    idx = jnp.sum(hit, axis=1, keepdims=True)
    idx = jnp.where(k_sub[:, :1] < n_sel, idx, 0.0)
    idx_ref[...] = jnp.broadcast_to(idx, (N_SELECT, LANES)).astype(jnp.int32)


def _nsa_decode_select(q, kc, *, past_len, n_blk):
    batch, kvh = q.shape[:2]
    nbc = kc.shape[3]
    return pl.pallas_call(
        functools.partial(_nsa_decode_select_body, past_len=past_len, n_blk=n_blk),
        grid=(batch, kvh),
        in_specs=[
            pl.BlockSpec((None, None, NSA_GROUP, NSA_HD), lambda b, h: (b, h, 0, 0)),
            pl.BlockSpec((2, None, None, nbc, NSA_HD), lambda b, h: (0, b, h, 0, 0)),
        ],
        out_specs=[
            pl.BlockSpec((None, None, NSA_GROUP, NSA_HD), lambda b, h: (b, h, 0, 0)),
            pl.BlockSpec((None, None, N_SELECT, LANES), lambda b, h: (b, h, 0, 0)),
        ],
        out_shape=[
            jax.ShapeDtypeStruct((batch, kvh, NSA_GROUP, NSA_HD), F32),
            jax.ShapeDtypeStruct((batch, kvh, N_SELECT, LANES), jnp.int32),
        ],
        compiler_params=_cparams(("parallel", "parallel")),
        name="nsa_decode_select",
    )(q, kc)


def _nsa_decode_attend_body(phys_ref, isnew_ref, q_ref, blk_ref, snew_ref, win_ref, wnew_ref,
                            gate_ref, oc_ref, o_ref, m_ref, l_ref, acc_ref, *, n_sel):
    b, h, k = pl.program_id(0), pl.program_id(1), pl.program_id(2)
    del phys_ref
    q = q_ref[...]
    qb = q.astype(BF16)

    @pl.when(k == 0)
    def _():
        m_ref[...] = jnp.full_like(m_ref, NEG_INF)
        l_ref[...] = jnp.zeros_like(l_ref)
        acc_ref[...] = jnp.zeros_like(acc_ref)

    def online(s, v_fn):
        m_old = m_ref[...]
        m_new = jnp.maximum(m_old, jnp.max(s, axis=-1, keepdims=True))
        alpha = jnp.exp(m_old - m_new)
        p = jnp.exp(s - m_new)
        l_ref[...] = alpha * l_ref[...] + jnp.sum(p, axis=-1, keepdims=True)
        acc_ref[...] = alpha * acc_ref[...] + v_fn(p)
        m_ref[...] = m_new

    @pl.when(isnew_ref[b, h, k] == 0)
    def _():
        kb = blk_ref[:, :NSA_HD].astype(BF16)
        vb = blk_ref[:, NSA_HD:].astype(BF16)
        online(_dot_nt(qb, kb) * ATTN_SCALE, lambda p: _dot(p.astype(BF16), vb))

    @pl.when(isnew_ref[b, h, k] != 0)
    def _():
        kn = snew_ref[:, :NSA_HD]
        vn = snew_ref[:, NSA_HD:]
        s = jnp.sum(q * kn, axis=-1, keepdims=True) * ATTN_SCALE
        online(s, lambda p: p * vn)

    @pl.when(k == n_sel - 1)
    def _():
        o_s = acc_ref[...] / jnp.maximum(l_ref[...], 1e-30)
        wb = win_ref.shape[0]
        kw = win_ref[:, :NSA_HD].astype(BF16)
        vw = win_ref[:, NSA_HD:].astype(BF16)
        prow = lax.broadcasted_iota(jnp.int32, (1, wb), 1)
        wmask = prow >= 1
        s = jnp.where(wmask, _dot_nt(qb, kw) * ATTN_SCALE, NEG_INF)
        s_n = jnp.sum(q * wnew_ref[:, :NSA_HD], axis=-1, keepdims=True) * ATTN_SCALE
        mx = jnp.maximum(jnp.max(s, axis=-1, keepdims=True), s_n)
        e = jnp.where(wmask, jnp.exp(s - mx), 0.0)
        e_n = jnp.exp(s_n - mx)
        den = jnp.maximum(jnp.sum(e, axis=-1, keepdims=True) + e_n, 1e-30)
        o_w = (_dot(e.astype(BF16), vw) + e_n * wnew_ref[:, NSA_HD:]) / den
        gates = _sigmoid(gate_ref[...])
        o_ref[...] = gates[0] * oc_ref[...] + gates[1] * o_s + gates[2] * o_w


def _nsa_decode_attend(phys, isnew, q, pool, snew, win, wnew, gate, o_c, *, layer):
    batch, kvh, n_sel = phys.shape
    kv2 = 2 * NSA_HD
    wb = win.shape[2]
    grid_spec = pltpu.PrefetchScalarGridSpec(
        num_scalar_prefetch=2,
        grid=(batch, kvh, n_sel),
        in_specs=[
            pl.BlockSpec((None, None, NSA_GROUP, NSA_HD), lambda b, h, k, ph, nw: (b, h, 0, 0)),
            pl.BlockSpec((None, None, BLOCK, kv2), lambda b, h, k, ph, nw: (layer, ph[b, h, k], 0, h)),
            pl.BlockSpec((None, None, 1, kv2), lambda b, h, k, ph, nw: (b, h, 0, 0)),
            pl.BlockSpec((None, None, wb, kv2), lambda b, h, k, ph, nw: (layer, b, 0, h)),
            pl.BlockSpec((None, None, 1, kv2), lambda b, h, k, ph, nw: (b, h, 0, 0)),
            pl.BlockSpec((None, None, 3, NSA_GROUP, NSA_HD), lambda b, h, k, ph, nw: (b, h, 0, 0, 0)),
            pl.BlockSpec((None, None, NSA_GROUP, NSA_HD), lambda b, h, k, ph, nw: (b, h, 0, 0)),
        ],
        out_specs=pl.BlockSpec((None, None, NSA_GROUP, NSA_HD), lambda b, h, k, ph, nw: (b, h, 0, 0)),
        scratch_shapes=[
            pltpu.VMEM((NSA_GROUP, 1), F32),
            pltpu.VMEM((NSA_GROUP, 1), F32),
            pltpu.VMEM((NSA_GROUP, NSA_HD), F32),
        ],
    )
    return pl.pallas_call(
        functools.partial(_nsa_decode_attend_body, n_sel=n_sel),
        grid_spec=grid_spec,
        out_shape=jax.ShapeDtypeStruct((batch, kvh, NSA_GROUP, NSA_HD), F32),
        compiler_params=_cparams(("parallel", "parallel", "arbitrary")),
        name="nsa_decode_attend",
    )(phys, isnew, q, pool, snew, win, wnew, gate, o_c)


def kernel(x_prompt, x_sample, cache_cmp_kv, cache_sel_kv, state_win_kv, state_hgrn, page_table,
           w_in, w_out, w_up, w_down, cmp_pe, cmp_w, lb_logits, hgrn_norm,
           norm_pre_mix, norm_post_mix, norm_pre_ffn, norm_post_ffn):
    depth = w_in.shape[0]
    bp, tp, d_model = x_prompt.shape
    bs, ts, _ = x_sample.shape
    assert ts == 1, "decode path handles one new token per sequence"
    d_hgrn = hgrn_norm.shape[1]
    heads = d_hgrn // HGRN_DK
    d_nsa = w_out.shape[1] - d_hgrn
    kvh = d_nsa // NSA_HD // NSA_GROUP
    kv_w = kvh * 2 * NSA_HD
    n_gate = 3 * kvh * NSA_GROUP
    n_main = 4 * d_hgrn + d_nsa + 3 * kv_w
    assert w_in.shape[2] == n_main + n_gate
    col_q = 4 * d_hgrn
    col_ckv = col_q + d_nsa
    col_skv = col_ckv + kv_w
    col_wkv = col_skv + kv_w
    col_gate = n_main
    n_pad = n_main + kvh * LANES
    past_len = page_table.shape[1] * PAGE_SIZE
    wb = state_win_kv.shape[2]
    assert wb == WINDOW and past_len % BLOCK == 0
    nb_p = tp // BLOCK
    nb_past = past_len // BLOCK
    n_blk_s = nb_past + 1

    p_lb = jax.nn.softmax(lb_logits.astype(F32), axis=0)
    lbs = jnp.cumsum(p_lb, axis=0) - p_lb[0:1]

    def prep_w_in(w):
        gate = w[:, n_main:].reshape(d_model, 3, kvh, NSA_GROUP).transpose(0, 2, 1, 3)
        gate = gate.reshape(d_model, kvh, 3 * NSA_GROUP)
        gate = jnp.pad(gate, ((0, 0), (0, 0), (0, LANES - 3 * NSA_GROUP))).reshape(d_model, kvh * LANES)
        return jnp.concatenate([w[:, :n_main], gate], axis=1).astype(BF16)

    def lb_rows(lb):
        lb = lb.reshape(heads, 1, HGRN_DK)
        rows = jnp.concatenate([jnp.log(jnp.maximum(lb, LB_FLOOR)), jnp.log1p(-lb), 1.0 - lb], axis=1)
        return jnp.pad(rows, ((0, 0), (0, 5), (0, 0)))

    blk_of_key = jnp.arange(tp) // BLOCK
    nbp = -(-nb_p // LANES) * LANES
    expand = (jnp.arange(nbp)[:, None] == blk_of_key[None, :]).astype(BF16)
    pool_cmp = cache_cmp_kv.reshape(depth, cache_cmp_kv.shape[1], PAGE_SIZE, kv_w)
    pool_sel = cache_sel_kv.reshape(depth, cache_sel_kv.shape[1] * (PAGE_SIZE // BLOCK), BLOCK, kv_w)
    win_state = state_win_kv.reshape(depth, bs, wb, kv_w)

    xp = x_prompt.reshape(bp * tp, d_model)
    xs = x_sample.reshape(bs, d_model)
    outs = {k: [] for k in ("p_cmp", "p_sel", "p_win", "p_rec", "s_cmp", "s_sel", "s_win", "s_rec")}
    kv5 = lambda a, b, t: a.reshape(b, t, kvh, 2, NSA_HD)

    for l in range(depth):
        w_in_l = prep_w_in(w_in[l])
        w_out_l = w_out[l].astype(BF16)
        w_up_l = w_up[l].astype(BF16)
        w_down_l = w_down[l].astype(BF16)
        w_c = cmp_w[l].reshape(2, BLOCK * NSA_HD, NSA_HD).astype(BF16)
        pe_c = cmp_pe[l].transpose(1, 0, 2)
        lbp = lb_rows(lbs[l])
        gn = hgrn_norm[l].reshape(heads, 1, HGRN_DK)

        u = _norm_matmul(xp, norm_pre_mix[l], w_in_l, act=False, out_dtype=F32)
        o_r, st = _hgrn_prompt(u, lbp, gn, batch=bp, seq=tp, heads=heads, tt=min(tp, 256))
        ckv = u[:, col_ckv:col_skv]
        skv = u[:, col_skv:col_wkv]
        wkv = u[:, col_wkv:col_gate]
        xc = ckv.reshape(bp, nb_p, BLOCK, kvh, 2, NSA_HD).transpose(4, 0, 3, 1, 2, 5)
        kc = _compress(xc.reshape(2, bp * kvh * nb_p, BLOCK * NSA_HD),
                       pe_c.reshape(2, 1, BLOCK * NSA_HD), w_c)
        kc = kc.reshape(2, bp, kvh, nb_p, NSA_HD).transpose(1, 2, 0, 3, 4)
        kc = jnp.pad(kc, ((0, 0), (0, 0), (0, 0), (0, nbp - nb_p), (0, 0))).astype(BF16)
        o_b = _nsa_prompt(u, kc, expand, batch=bp, seq=tp, kvh=kvh,
                          col_q=col_q, col_skv=col_skv, col_wkv=col_wkv, col_gate=col_gate)
        mix = jnp.concatenate([o_r, o_b], axis=1)
        hmid = _matmul_norm_res(mix, w_out_l, norm_post_mix[l], xp)
        f = _norm_matmul(hmid, norm_pre_ffn[l], w_up_l, act=True, out_dtype=BF16)
        xp = _matmul_norm_res(f, w_down_l, norm_post_ffn[l], hmid)
        outs["p_cmp"].append(kv5(ckv, bp, tp))
        outs["p_sel"].append(kv5(skv, bp, tp))
        outs["p_win"].append(kv5(wkv, bp, tp)[:, tp - min(WINDOW, tp):])
        outs["p_rec"].append(st)

        us = _norm_matmul(xs, norm_pre_mix[l], w_in_l, act=False, out_dtype=F32)
        o_r, st = _hgrn_decode(us, lbp, gn, state_hgrn[l], batch=bs, heads=heads)
        q_s = us[:, col_q:col_ckv].reshape(bs, kvh, NSA_GROUP, NSA_HD)
        ckv = us[:, col_ckv:col_skv]
        skv = us[:, col_skv:col_wkv]
        wkv = us[:, col_wkv:col_gate]
        kc_s = _compress_paged(pool_cmp, page_table, pe_c, cmp_w[l].astype(BF16), layer=l, kvh=kvh)
        o_c, idx = _nsa_decode_select(q_s, kc_s, past_len=past_len, n_blk=n_blk_s)
        idx = idx[..., 0]
        ppos = jnp.minimum(idx, nb_past - 1) * BLOCK
        page = jnp.take_along_axis(page_table, (ppos // PAGE_SIZE).reshape(bs, -1), axis=1)
        phys = page.reshape(idx.shape) * (PAGE_SIZE // BLOCK) + (ppos % PAGE_SIZE) // BLOCK
        isnew = (idx >= nb_past).astype(jnp.int32)
        gate_s = us[:, col_gate:].reshape(bs, kvh, LANES)[:, :, :3 * NSA_GROUP]
        gate_s = jnp.broadcast_to(gate_s.reshape(bs, kvh, 3, NSA_GROUP, 1), (bs, kvh, 3, NSA_GROUP, NSA_HD))
        o_b = _nsa_decode_attend(phys.astype(jnp.int32), isnew, q_s, pool_sel,
                                 skv.reshape(bs, kvh, 1, 2 * NSA_HD), win_state,
                                 wkv.reshape(bs, kvh, 1, 2 * NSA_HD), gate_s, o_c, layer=l)
        mix = jnp.concatenate([o_r, o_b.reshape(bs, d_nsa)], axis=1).astype(BF16)
        hmid = _matmul_norm_res(mix, w_out_l, norm_post_mix[l], xs)
        f = _norm_matmul(hmid, norm_pre_ffn[l], w_up_l, act=True, out_dtype=BF16)
        xs = _matmul_norm_res(f, w_down_l, norm_post_ffn[l], hmid)
        outs["s_cmp"].append(kv5(ckv, bs, ts))
        outs["s_sel"].append(kv5(skv, bs, ts))
        outs["s_win"].append(jnp.concatenate([state_win_kv[l][:, ts:], kv5(wkv, bs, ts)], axis=1))
        outs["s_rec"].append(st)

    stack = lambda k: jnp.stack(outs[k])
    return (xp.reshape(bp, tp, d_model), xs.reshape(bs, ts, d_model),
            stack("p_cmp"), stack("p_sel"), stack("p_win"), stack("p_rec"),
            stack("s_cmp"), stack("s_sel"), stack("s_win"), stack("s_rec"))
```

```python
import functools
import math

import jax
import jax.numpy as jnp
from jax import lax
from jax.experimental import pallas as pl
from jax.experimental.pallas import tpu as pltpu

F32 = jnp.float32
BF16 = jnp.bfloat16

HGRN_DK = 128
NSA_HD = 128
NSA_GROUP = 4
BLOCK = 64
N_SELECT = 16
WINDOW = 512
PAGE_SIZE = 128
EPS = 1e-6
NEG_INF = -1e30
LB_FLOOR = 1e-30
ATTN_SCALE = NSA_HD ** -0.5
FORCE_SCORE = NSA_GROUP + 1.0

LANES = 128
VMEM_LIMIT = 56 * 1024 * 1024

HGRN_CHUNK = 64
HGRN_SUB = 16
HGRN_HEADS_PER_STEP = 4
PAGES_PER_STEP = 16
KEY_CHUNK = 512


def _cparams(sem):
    return pltpu.CompilerParams(dimension_semantics=sem, vmem_limit_bytes=VMEM_LIMIT)


def _dot(a, b):
    return jnp.dot(a, b, preferred_element_type=F32)


def _dot_nt(a, b):
    return lax.dot_general(a, b, (((1,), (1,)), ((), ())), preferred_element_type=F32)


def _dot_tn(a, b):
    return lax.dot_general(a, b, (((0,), (0,)), ((), ())), preferred_element_type=F32)


def _sigmoid(x):
    return 1.0 / (1.0 + jnp.exp(-x))


def _silu(x):
    return x * _sigmoid(x)


def _rms(x, g):
    ms = jnp.mean(x * x, axis=-1, keepdims=True)
    return x * lax.rsqrt(ms + EPS) * g


def _masked_softmax(s, mask):
    s = jnp.where(mask, s, NEG_INF)
    e = jnp.where(mask, jnp.exp(s - jnp.max(s, axis=-1, keepdims=True)), 0.0)
    return e / jnp.maximum(jnp.sum(e, axis=-1, keepdims=True), 1e-30)


def _rmsnorm_body(x_ref, g_ref, o_ref):
    o_ref[...] = _rms(x_ref[...], g_ref[...]).astype(o_ref.dtype)


def _rmsnorm_bf16(x, g):
    m, d = x.shape
    tm = min(m, 256)
    return pl.pallas_call(
        _rmsnorm_body,
        grid=(m // tm,),
        in_specs=[pl.BlockSpec((tm, d), lambda i: (i, 0)), pl.BlockSpec((1, d), lambda i: (0, 0))],
        out_specs=pl.BlockSpec((tm, d), lambda i: (i, 0)),
        out_shape=jax.ShapeDtypeStruct((m, d), BF16),
        compiler_params=_cparams(("parallel",)),
        name="rmsnorm",
    )(x, g.reshape(1, d))


def _norm_res_body(y_ref, g1_ref, r_ref, *rest):
    out = r_ref[...] + _rms(y_ref[...], g1_ref[...])
    if len(rest) == 1:
        rest[0][...] = out
    else:
        g2_ref, o_ref, n_ref = rest
        o_ref[...] = out
        n_ref[...] = _rms(out, g2_ref[...]).astype(n_ref.dtype)


def _norm_res(y, g1, res, g2=None):
    m, d = y.shape
    tm = min(m, 256)
    row = pl.BlockSpec((tm, d), lambda i: (i, 0))
    vec = pl.BlockSpec((1, d), lambda i: (0, 0))
    if g2 is None:
        return pl.pallas_call(
            _norm_res_body, grid=(m // tm,), in_specs=[row, vec, row], out_specs=row,
            out_shape=jax.ShapeDtypeStruct((m, d), F32),
            compiler_params=_cparams(("parallel",)), name="norm_res",
        )(y, g1.reshape(1, d), res)
    return pl.pallas_call(
        _norm_res_body, grid=(m // tm,), in_specs=[row, vec, row, vec], out_specs=[row, row],
        out_shape=[jax.ShapeDtypeStruct((m, d), F32), jax.ShapeDtypeStruct((m, d), BF16)],
        compiler_params=_cparams(("parallel",)), name="norm_res_norm",
    )(y, g1.reshape(1, d), res, g2.reshape(1, d))


def _matmul_body(x_ref, w_ref, o_ref, *, act, nk):
    part = _dot(x_ref[...], w_ref[...].astype(BF16))
    if nk == 1:
        if act:
            part = jnp.square(jnp.maximum(part, 0.0))
        o_ref[...] = part.astype(o_ref.dtype)
    else:
        kk = pl.program_id(2)

        @pl.when(kk == 0)
        def _():
            o_ref[...] = part

        @pl.when(kk > 0)
        def _():
            o_ref[...] += part


def _matmul(x, w, layer, n_out, *, act=False, out_dtype=F32):
    m, k = x.shape
    tm = min(m, 1024)
    tk, tn = (k, 512) if k <= 4096 else (2048, 1024)
    tn = math.gcd(tn, n_out)
    nk = k // tk
    assert m % tm == 0 and n_out % tn == 0 and k % tk == 0
    assert nk == 1 or (out_dtype == F32 and not act)
    return pl.pallas_call(
        functools.partial(_matmul_body, act=act, nk=nk),
        grid=(m // tm, n_out // tn, nk),
        in_specs=[
            pl.BlockSpec((tm, tk), lambda i, j, kk: (i, kk)),
            pl.BlockSpec((None, tk, tn), lambda i, j, kk: (layer, kk, j)),
        ],
        out_specs=pl.BlockSpec((tm, tn), lambda i, j, kk: (i, j)),
        out_shape=jax.ShapeDtypeStruct((m, n_out), out_dtype),
        compiler_params=_cparams(("parallel", "parallel", "arbitrary")),
        name="matmul",
    )(x, w)


def _hgrn_gates(z, lbp):
    a = lbp[0:1, :]
    log_sig = jnp.minimum(z, 0.0) - jnp.log1p(jnp.exp(-jnp.abs(z)))
    b = lbp[1:2, :] + log_sig
    logf = jnp.maximum(a, b) + jnp.log1p(jnp.exp(-jnp.abs(a - b)))
    k = lbp[2:3, :] * _sigmoid(-z)
    return logf, k


def _hgrn_out(o, gn, hg):
    return _rms(o, gn) * _silu(hg)


def _hgrn_chunk(q, z, v, hg, lbp, gn, st, consts):
    c, sub = HGRN_CHUNK, HGRN_SUB
    ns = c // sub
    tri, ones, t_in_sub = consts
    q = _silu(q)
    logf, k = _hgrn_gates(z, lbp)

    hi = logf.astype(BF16)
    r1 = logf - hi.astype(F32)
    mid = r1.astype(BF16)
    lo = (r1 - mid.astype(F32)).astype(BF16)
    cb = _dot(tri, hi) + _dot(tri, mid) + _dot(tri, lo)

    q3 = q.reshape(ns, sub, HGRN_DK)
    k3 = k.reshape(ns, sub, HGRN_DK)
    v3 = v.reshape(ns, sub, HGRN_DK)
    cb3 = cb.reshape(ns, sub, HGRN_DK)
    e_end = cb3[:, sub - 1:sub, :]
    b_beg = jnp.concatenate([jnp.zeros((1, 1, HGRN_DK), F32), e_end[:ns - 1]], axis=0)
    qt = q3 * jnp.exp(cb3 - b_beg)
    kt = k3 * jnp.exp(e_end - cb3)

    q_cols, k_cols = [], []
    zeros_sub = jnp.zeros((sub, HGRN_DK), F32)
    for j in range(ns - 1):
        qc, kc = [], []
        for i in range(ns):
            qc.append(qt[i] * jnp.exp(b_beg[i] - e_end[j]) if i > j else zeros_sub)
            kc.append(kt[i] if i == j else zeros_sub)
        q_cols.append(jnp.concatenate(qc, axis=0))
        k_cols.append(jnp.concatenate(kc, axis=0))
    q_hat = jnp.concatenate(q_cols, axis=1).astype(BF16)
    k_hat = jnp.concatenate(k_cols, axis=1).astype(BF16)
    att_off = _dot_nt(q_hat, k_hat)
    o = _dot(att_off.astype(BF16), v.astype(BF16))

    parts = []
    for s in range(sub):
        dec = jnp.exp(cb3 - cb3[:, s:s + 1, :])
        p = jnp.where(t_in_sub >= s, q3 * k3[:, s:s + 1, :] * dec, 0.0)
        parts.append(p.reshape(c, HGRN_DK).astype(BF16))
    rsum = _dot(jnp.concatenate(parts, axis=0), ones)
    o_diag = jnp.zeros((ns, sub, HGRN_DK), F32)
    for s in range(sub):
        o_diag = o_diag + rsum[s * c:(s + 1) * c, :].reshape(ns, sub, HGRN_DK) * v3[:, s:s + 1, :]
    o = o + o_diag.reshape(c, HGRN_DK)

    o = o + _dot_nt((q * jnp.exp(cb)).astype(BF16), st.astype(BF16))
    last = cb[c - 1:c, :]
    kd = (k * jnp.exp(last - cb)).astype(BF16)
    st_new = st * jnp.exp(last) + _dot_tn(v.astype(BF16), kd)
    return _hgrn_out(o, gn, hg), st_new


def _hgrn_prompt_body(q_ref, f_ref, i_ref, g_ref, lb_ref, gn_ref, o_ref, st_ref, s_ref, *, n_chunks):
    c, sub = HGRN_CHUNK, HGRN_SUB
    hp = s_ref.shape[0]
    tstep = pl.program_id(2)

    @pl.when(tstep == 0)
    def _():
        s_ref[...] = jnp.zeros_like(s_ref)

    row = lax.broadcasted_iota(jnp.int32, (c, c), 0)
    col = lax.broadcasted_iota(jnp.int32, (c, c), 1)
    consts = (
        jnp.where(row >= col, 1.0, 0.0).astype(BF16),
        jnp.ones((LANES, LANES), BF16),
        lax.broadcasted_iota(jnp.int32, (c // sub, sub, HGRN_DK), 1),
    )

    def chunk(ci, carry):
        rows = pl.ds(pl.multiple_of(ci * c, c), c)
        for hh in range(hp):
            cols = slice(hh * HGRN_DK, (hh + 1) * HGRN_DK)
            out, st_new = _hgrn_chunk(q_ref[rows, cols], f_ref[rows, cols], i_ref[rows, cols],
                                      g_ref[rows, cols], lb_ref[hh], gn_ref[hh], s_ref[hh], consts)
            s_ref[hh] = st_new
            o_ref[rows, cols] = out.astype(o_ref.dtype)
        return carry

    lax.fori_loop(0, n_chunks, chunk, 0)

    @pl.when(tstep == pl.num_programs(2) - 1)
    def _():
        for hh in range(hp):
            st_ref[hh] = s_ref[hh].T


def _hgrn_prompt(u, lbp, gn, *, batch, seq, heads, tt):
    nt = seq // tt
    hp = math.gcd(heads, HGRN_HEADS_PER_STEP)
    w = hp * HGRN_DK
    ng = heads // hp
    blk = lambda part: pl.BlockSpec((tt, w), lambda b, h, t: (b * nt + t, part * ng + h))
    return pl.pallas_call(
        functools.partial(_hgrn_prompt_body, n_chunks=tt // HGRN_CHUNK),
        grid=(batch, ng, nt),
        in_specs=[
            blk(0), blk(1), blk(2), blk(3),
            pl.BlockSpec((hp, 8, HGRN_DK), lambda b, h, t: (h, 0, 0)),
            pl.BlockSpec((hp, 1, HGRN_DK), lambda b, h, t: (h, 0, 0)),
        ],
        out_specs=[
            pl.BlockSpec((tt, w), lambda b, h, t: (b * nt + t, h)),
            pl.BlockSpec((None, hp, HGRN_DK, HGRN_DK), lambda b, h, t: (b, h, 0, 0)),
        ],
        out_shape=[
            jax.ShapeDtypeStruct((batch * seq, heads * HGRN_DK), BF16),
            jax.ShapeDtypeStruct((batch, heads, HGRN_DK, HGRN_DK), F32),
        ],
        scratch_shapes=[pltpu.VMEM((hp, HGRN_DK, HGRN_DK), F32)],
        compiler_params=_cparams(("parallel", "parallel", "arbitrary")),
        name="hgrn_prompt",
    )(u, u, u, u, lbp, gn)


def _hgrn_decode_body(q_ref, f_ref, i_ref, g_ref, lb_ref, gn_ref, s0_ref, o_ref, st_ref):
    r = pl.ds(pl.program_id(1), 1)
    logf, k = _hgrn_gates(f_ref[r, :], lb_ref[...])
    q = _silu(q_ref[r, :])
    v = i_ref[r, :]

    def col(x):
        return jnp.broadcast_to(x, (HGRN_DK, HGRN_DK)).T

    s_new = col(jnp.exp(logf)) * s0_ref[...] + col(k) * v
    st_ref[...] = s_new
    o = jnp.sum(col(q) * s_new, axis=0, keepdims=True)
    o_ref[r, :] = _hgrn_out(o, gn_ref[...], g_ref[r, :])


def _hgrn_decode(u, lbp, gn, s0, *, batch, heads):
    blk = lambda off: pl.BlockSpec((batch, HGRN_DK), lambda h, b: (0, off + h))
    return pl.pallas_call(
        _hgrn_decode_body,
        grid=(heads, batch),
        in_specs=[
            blk(0), blk(heads), blk(2 * heads), blk(3 * heads),
            pl.BlockSpec((None, 8, HGRN_DK), lambda h, b: (h, 0, 0)),
            pl.BlockSpec((None, 1, HGRN_DK), lambda h, b: (h, 0, 0)),
            pl.BlockSpec((None, None, HGRN_DK, HGRN_DK), lambda h, b: (b, h, 0, 0)),
        ],
        out_specs=[
            pl.BlockSpec((batch, HGRN_DK), lambda h, b: (0, h)),
            pl.BlockSpec((None, None, HGRN_DK, HGRN_DK), lambda h, b: (b, h, 0, 0)),
        ],
        out_shape=[
            jax.ShapeDtypeStruct((batch, heads * HGRN_DK), F32),
            jax.ShapeDtypeStruct(s0.shape, F32),
        ],
        compiler_params=_cparams(("parallel", "arbitrary")),
        name="hgrn_decode",
    )(u, u, u, u, lbp, gn, s0)


def _compress_body(x_ref, pe_ref, w_ref, o_ref):
    o_ref[...] = _dot((x_ref[...] + pe_ref[...]).astype(BF16), w_ref[...])


def _compress(x, pe, w):
    _, r, kdim = x.shape
    tm = min(r, 256)
    return pl.pallas_call(
        _compress_body,
        grid=(2, r // tm),
        in_specs=[
            pl.BlockSpec((None, tm, kdim), lambda c, i: (c, i, 0)),
            pl.BlockSpec((None, 1, kdim), lambda c, i: (c, 0, 0)),
            pl.BlockSpec((None, kdim, NSA_HD), lambda c, i: (c, 0, 0)),
        ],
        out_specs=pl.BlockSpec((None, tm, NSA_HD), lambda c, i: (c, i, 0)),
        out_shape=jax.ShapeDtypeStruct((2, r, NSA_HD), F32),
        compiler_params=_cparams(("parallel", "parallel")),
        name="compress",
    )(x, pe, w)


def _compress_paged_body(pt_ref, *refs, kvh):
    del pt_ref
    npg = PAGES_PER_STEP
    page_refs = refs[:npg]
    pe_ref, w_ref, o_ref, buf_ref, lhs_ref = refs[npg:]
    rpp = PAGE_SIZE * kvh * 2
    for p in range(npg):
        buf_ref[p * rpp:(p + 1) * rpp, :] = page_refs[p][...]
    nblk = npg * PAGE_SIZE // BLOCK
    blk_stride = BLOCK * kvh * 2
    for c in range(2):
        for r in range(BLOCK):
            pieces = [buf_ref[pl.ds(r * kvh * 2 + 2 * h + c, nblk, stride=blk_stride), :] for h in range(kvh)]
            x = jnp.concatenate(pieces, axis=0) + pe_ref[c, r:r + 1, :]
            lhs_ref[c, :, r * NSA_HD:(r + 1) * NSA_HD] = x.astype(BF16)
        o_ref[c] = _dot(lhs_ref[c], w_ref[c]).reshape(kvh, nblk, NSA_HD)


def _compress_paged(pool, page_table, pe, w, *, layer, kvh):
    batch, n_pages = page_table.shape
    npg = PAGES_PER_STEP
    rpp = pool.shape[2]
    nblk = npg * PAGE_SIZE // BLOCK

    def page_spec(p):
        return pl.BlockSpec((None, None, rpp, NSA_HD),
                            lambda b, g, pt: (layer, pt[b, g * npg + p], 0, 0))

    grid_spec = pltpu.PrefetchScalarGridSpec(
        num_scalar_prefetch=1,
        grid=(batch, n_pages // npg),
        in_specs=[page_spec(p) for p in range(npg)] + [
            pl.BlockSpec((2, BLOCK, NSA_HD), lambda b, g, pt: (0, 0, 0)),
            pl.BlockSpec((2, BLOCK * NSA_HD, NSA_HD), lambda b, g, pt: (0, 0, 0)),
        ],
        out_specs=pl.BlockSpec((2, None, kvh, nblk, NSA_HD), lambda b, g, pt: (0, b, 0, g, 0)),
        scratch_shapes=[
            pltpu.VMEM((npg * rpp, NSA_HD), F32),
            pltpu.VMEM((2, kvh * nblk, BLOCK * NSA_HD), BF16),
        ],
    )
    return pl.pallas_call(
        functools.partial(_compress_paged_body, kvh=kvh),
        grid_spec=grid_spec,
        out_shape=jax.ShapeDtypeStruct((2, batch, kvh, n_pages * PAGE_SIZE // BLOCK, NSA_HD), F32),
        compiler_params=_cparams(("parallel", "arbitrary")),
        name="compress_paged",
    )(page_table, *([pool] * npg), pe, w)


def _nsa_prompt_body(q_ref, kc_ref, skv_ref, wkv_ref, gate_ref, e_ref, o_ref,
                     ks_ref, vs_ref, kw_ref, vw_ref, m_ref, l_ref, acc_ref, *, tq, seq, n_blk, span, kc_len):
    i = pl.program_id(2)

    @pl.when(i == 0)
    def _():
        ks_ref[...] = skv_ref[:, :NSA_HD].astype(BF16)
        vs_ref[...] = skv_ref[:, NSA_HD:].astype(BF16)
        kw_ref[...] = wkv_ref[:, :NSA_HD].astype(BF16)
        vw_ref[...] = wkv_ref[:, NSA_HD:].astype(BF16)

    nbp = kc_ref.shape[1]
    nb8 = -(-n_blk // 8) * 8
    n_sel = min(N_SELECT, n_blk)
    q0 = i * tq
    qpos = q0 + lax.broadcasted_iota(jnp.int32, (tq, 1), 0)
    cur = qpos // BLOCK
    n_lane = lax.broadcasted_iota(jnp.int32, (1, nbp), 1)
    cmask = (n_lane * BLOCK + (BLOCK - 1) <= qpos) & (n_lane < seq // BLOCK)

    kc = kc_ref[0]
    vc = kc_ref[1]
    qs = [(q_ref[:, g * NSA_HD:(g + 1) * NSA_HD] * ATTN_SCALE).astype(BF16) for g in range(NSA_GROUP)]

    o_c = []
    imp = jnp.zeros((tq, nbp), F32)
    for g in range(NSA_GROUP):
        p = _masked_softmax(_dot_nt(qs[g], kc), cmask)
        o_c.append(_dot(p.astype(BF16), vc))
        imp = imp + p

    forced = (n_lane == 0) | (n_lane == cur) | (n_lane == cur - 1)
    score = jnp.where(n_lane <= cur, jnp.where(forced, FORCE_SCORE, imp), -1.0)
    score = jnp.where(n_lane < n_blk, score, -2.0)
    score_t = score.T[:nb8, :]
    j_idx = lax.broadcasted_iota(jnp.int32, (nb8, tq), 0)
    cnt = jnp.zeros((nb8, tq), F32)
    for j in range(n_blk):
        rj = score_t[j:j + 1, :]
        beats = jnp.where(rj > score_t, 1.0, jnp.where((rj == score_t) & (j_idx > j), 1.0, 0.0))
        cnt = cnt + beats
    cur_t = (q0 + lax.broadcasted_iota(jnp.int32, (1, tq), 1)) // BLOCK
    sel_t = jnp.where((cnt < n_sel) & (j_idx <= cur_t) & (j_idx < n_blk), 1.0, 0.0)
    if nbp > nb8:
        sel_t = jnp.concatenate([sel_t, jnp.zeros((nbp - nb8, tq), F32)], axis=0)
    sel = sel_t.T.astype(BF16)

    m_ref[...] = jnp.full_like(m_ref, NEG_INF)
    l_ref[...] = jnp.zeros_like(l_ref)
    acc_ref[...] = jnp.zeros_like(acc_ref)

    def key_chunk(ci, carry):
        k0 = pl.multiple_of(ci * kc_len, kc_len)
        kk = ks_ref[pl.ds(k0, kc_len), :]
        vv = vs_ref[pl.ds(k0, kc_len), :]
        kpos = k0 + lax.broadcasted_iota(jnp.int32, (1, kc_len), 1)
        bias = jnp.where((_dot(sel, e_ref[ci]) > 0.5) & (kpos <= qpos), 0.0, NEG_INF)
        for g in range(NSA_GROUP):
            s = _dot_nt(qs[g], kk) + bias
            m_old = m_ref[g]
            m_new = jnp.maximum(m_old, jnp.max(s, axis=-1, keepdims=True))
            alpha = jnp.exp(m_old - m_new)
            p = jnp.exp(s - m_new)
            l_ref[g] = alpha * l_ref[g] + jnp.sum(p, axis=-1, keepdims=True)
            acc_ref[g] = alpha * acc_ref[g] + _dot(p.astype(BF16), vv)
            m_ref[g] = m_new
        return carry

    lax.fori_loop(0, (q0 + tq + kc_len - 1) // kc_len, key_chunk, 0)

    start = pl.multiple_of(jnp.maximum(q0 + tq - span, 0), tq)
    wpos = start + lax.broadcasted_iota(jnp.int32, (1, span), 1)
    dist = qpos - wpos
    wbias = jnp.where((dist >= 0) & (dist < WINDOW), 0.0, NEG_INF)
    kw = kw_ref[pl.ds(start, span), :]
    vw = vw_ref[pl.ds(start, span), :]

    gates = _sigmoid(gate_ref[...])
    for g in range(NSA_GROUP):
        o_s = acc_ref[g] / l_ref[g]
        s = _dot_nt(qs[g], kw) + wbias
        e = jnp.exp(s - jnp.max(s, axis=-1, keepdims=True))
        o_w = _dot(e.astype(BF16), vw) / jnp.sum(e, axis=-1, keepdims=True)
        gc = gates[:, g:g + 1]
        gs = gates[:, NSA_GROUP + g:NSA_GROUP + g + 1]
        gw = gates[:, 2 * NSA_GROUP + g:2 * NSA_GROUP + g + 1]
        o = gc * o_c[g] + gs * o_s + gw * o_w
        o_ref[:, g * NSA_HD:(g + 1) * NSA_HD] = o.astype(o_ref.dtype)


def _nsa_prompt(u, ug, kc, expand, *, batch, seq, kvh, col_q, col_skv, col_wkv):
    tq = min(seq, 128)
    nq = seq // tq
    span = min(seq, WINDOW + tq)
    qw = NSA_GROUP * NSA_HD
    nbp = kc.shape[3]
    n_kc, _, kc_len = expand.shape
    return pl.pallas_call(
        functools.partial(_nsa_prompt_body, tq=tq, seq=seq, n_blk=seq // BLOCK, span=span, kc_len=kc_len),
        grid=(batch, kvh, nq),
        in_specs=[
            pl.BlockSpec((tq, qw), lambda b, h, i: (b * nq + i, col_q // qw + h)),
            pl.BlockSpec((None, None, 2, nbp, NSA_HD), lambda b, h, i: (b, h, 0, 0, 0)),
            pl.BlockSpec((seq, 2 * NSA_HD), lambda b, h, i: (b, col_skv // (2 * NSA_HD) + h)),
            pl.BlockSpec((seq, 2 * NSA_HD), lambda b, h, i: (b, col_wkv // (2 * NSA_HD) + h)),
            pl.BlockSpec((tq, LANES), lambda b, h, i: (b * nq + i, h)),
            pl.BlockSpec((n_kc, nbp, kc_len), lambda b, h, i: (0, 0, 0)),
        ],
        out_specs=pl.BlockSpec((tq, qw), lambda b, h, i: (b * nq + i, h)),
        out_shape=jax.ShapeDtypeStruct((batch * seq, kvh * qw), BF16),
        scratch_shapes=[pltpu.VMEM((seq, NSA_HD), BF16)] * 4 + [
            pltpu.VMEM((NSA_GROUP, tq, 1), F32),
            pltpu.VMEM((NSA_GROUP, tq, 1), F32),
            pltpu.VMEM((NSA_GROUP, tq, NSA_HD), F32),
        ],
        compiler_params=_cparams(("parallel", "parallel", "arbitrary")),
        name="nsa_prompt",
    )(u, kc, u, u, ug, expand)


def _nsa_decode_select_body(q_ref, kc_ref, oc_ref, idx_ref, *, past_len, n_blk):
    nbc = kc_ref.shape[1]
    npad = -(-n_blk // LANES) * LANES
    n_sel = min(N_SELECT, n_blk)
    qpos = past_len
    cur = qpos // BLOCK
    q = q_ref[...].astype(BF16)
    n_lane = lax.broadcasted_iota(jnp.int32, (1, nbc), 1)
    cmask = n_lane * BLOCK + (BLOCK - 1) <= qpos
    p = _masked_softmax(_dot_nt(q, kc_ref[0].astype(BF16)) * ATTN_SCALE, cmask)
    oc_ref[...] = _dot(p.astype(BF16), kc_ref[1].astype(BF16))
    imp = jnp.sum(p, axis=0, keepdims=True)
    imp = jnp.concatenate([imp, jnp.zeros((1, npad - nbc), F32)], axis=1)

    j_lane = lax.broadcasted_iota(jnp.int32, (1, npad), 1)
    forced = (j_lane == 0) | (j_lane == cur) | (j_lane == cur - 1)
    score = jnp.where(j_lane <= cur, jnp.where(forced, FORCE_SCORE, imp), -1.0)
    score = jnp.where(j_lane < n_blk, score, -2.0)
    s_row = jnp.broadcast_to(score, (npad, npad))
    s_col = s_row.T
    jp = lax.broadcasted_iota(jnp.int32, (npad, npad), 0)
    jj = lax.broadcasted_iota(jnp.int32, (npad, npad), 1)
    beats = jnp.where(s_col > s_row, 1.0, jnp.where((s_col == s_row) & (jp < jj), 1.0, 0.0))
    rank = jnp.sum(beats, axis=0, keepdims=True)
    k_sub = lax.broadcasted_iota(jnp.int32, (N_SELECT, npad), 0).astype(F32)
    j_f = lax.broadcasted_iota(jnp.int32, (N_SELECT, npad), 1).astype(F32)
    hit = jnp.where((rank == k_sub) & (j_f < n_blk), j_f, 0.0)
    idx = jnp.sum(hit, axis=1, keepdims=True)
    idx = jnp.where(k_sub[:, :1] < n_sel, idx, 0.0)
    idx_ref[...] = jnp.broadcast_to(idx, (N_SELECT, LANES)).astype(jnp.int32)


def _nsa_decode_select(q, kc, *, past_len, n_blk):
    batch, kvh = q.shape[:2]
    nbc = kc.shape[3]
    return pl.pallas_call(
        functools.partial(_nsa_decode_select_body, past_len=past_len, n_blk=n_blk),
        grid=(batch, kvh),
        in_specs=[
            pl.BlockSpec((None, None, NSA_GROUP, NSA_HD), lambda b, h: (b, h, 0, 0)),
            pl.BlockSpec((2, None, None, nbc, NSA_HD), lambda b, h: (0, b, h, 0, 0)),
        ],
        out_specs=[
            pl.BlockSpec((None, None, NSA_GROUP, NSA_HD), lambda b, h: (b, h, 0, 0)),
            pl.BlockSpec((None, None, N_SELECT, LANES), lambda b, h: (b, h, 0, 0)),
        ],
        out_shape=[
            jax.ShapeDtypeStruct((batch, kvh, NSA_GROUP, NSA_HD), F32),
            jax.ShapeDtypeStruct((batch, kvh, N_SELECT, LANES), jnp.int32),
        ],
        compiler_params=_cparams(("parallel", "parallel")),
        name="nsa_decode_select",
    )(q, kc)


def _nsa_decode_attend_body(phys_ref, isnew_ref, q_ref, blk_ref, snew_ref, win_ref, wnew_ref,
                            gate_ref, oc_ref, o_ref, m_ref, l_ref, acc_ref, *, n_sel, kvh):
    b, h, k = pl.program_id(0), pl.program_id(1), pl.program_id(2)
    del phys_ref
    q = q_ref[...]
    qb = q.astype(BF16)
    rstride = 2 * kvh

    @pl.when(k == 0)
    def _():
        m_ref[...] = jnp.full_like(m_ref, NEG_INF)
        l_ref[...] = jnp.zeros_like(l_ref)
        acc_ref[...] = jnp.zeros_like(acc_ref)

    def online(s, v_fn):
        m_old = m_ref[...]
        m_new = jnp.maximum(m_old, jnp.max(s, axis=-1, keepdims=True))
        alpha = jnp.exp(m_old - m_new)
        p = jnp.exp(s - m_new)
        l_ref[...] = alpha * l_ref[...] + jnp.sum(p, axis=-1, keepdims=True)
        acc_ref[...] = alpha * acc_ref[...] + v_fn(p)
        m_ref[...] = m_new

    @pl.when(isnew_ref[b, h, k] == 0)
    def _():
        kb = blk_ref[pl.ds(2 * h, BLOCK, stride=rstride), :].astype(BF16)
        vb = blk_ref[pl.ds(2 * h + 1, BLOCK, stride=rstride), :].astype(BF16)
        online(_dot_nt(qb, kb) * ATTN_SCALE, lambda p: _dot(p.astype(BF16), vb))

    @pl.when(isnew_ref[b, h, k] != 0)
    def _():
        kn = snew_ref[:, :NSA_HD]
        vn = snew_ref[:, NSA_HD:]
        s = jnp.sum(q * kn, axis=-1, keepdims=True) * ATTN_SCALE
        online(s, lambda p: p * vn)

    @pl.when(k == n_sel - 1)
    def _():
        o_s = acc_ref[...] / jnp.maximum(l_ref[...], 1e-30)
        wb = win_ref.shape[0] // rstride
        kw = win_ref[pl.ds(2 * h, wb, stride=rstride), :].astype(BF16)
        vw = win_ref[pl.ds(2 * h + 1, wb, stride=rstride), :].astype(BF16)
        prow = lax.broadcasted_iota(jnp.int32, (1, wb), 1)
        wmask = prow >= 1
        s = jnp.where(wmask, _dot_nt(qb, kw) * ATTN_SCALE, NEG_INF)
        s_n = jnp.sum(q * wnew_ref[:, :NSA_HD], axis=-1, keepdims=True) * ATTN_SCALE
        mx = jnp.maximum(jnp.max(s, axis=-1, keepdims=True), s_n)
        e = jnp.where(wmask, jnp.exp(s - mx), 0.0)
        e_n = jnp.exp(s_n - mx)
        den = jnp.maximum(jnp.sum(e, axis=-1, keepdims=True) + e_n, 1e-30)
        o_w = (_dot(e.astype(BF16), vw) + e_n * wnew_ref[:, NSA_HD:]) / den
        gates = _sigmoid(gate_ref[...])
        o_ref[...] = gates[0] * oc_ref[...] + gates[1] * o_s + gates[2] * o_w


def _nsa_decode_attend(phys, isnew, q, pool, snew, win, wnew, gate, o_c, *, layer):
    batch, kvh, n_sel = phys.shape
    kv2 = 2 * NSA_HD
    brows = pool.shape[2]
    wrows = win.shape[2]
    grid_spec = pltpu.PrefetchScalarGridSpec(
        num_scalar_prefetch=2,
        grid=(batch, kvh, n_sel),
        in_specs=[
            pl.BlockSpec((None, None, NSA_GROUP, NSA_HD), lambda b, h, k, ph, nw: (b, h, 0, 0)),
            pl.BlockSpec((None, None, brows, NSA_HD), lambda b, h, k, ph, nw: (layer, ph[b, h, k], 0, 0)),
            pl.BlockSpec((None, None, 1, kv2), lambda b, h, k, ph, nw: (b, h, 0, 0)),
            pl.BlockSpec((None, None, wrows, NSA_HD), lambda b, h, k, ph, nw: (layer, b, 0, 0)),
            pl.BlockSpec((None, None, 1, kv2), lambda b, h, k, ph, nw: (b, h, 0, 0)),
            pl.BlockSpec((None, None, 3, NSA_GROUP, NSA_HD), lambda b, h, k, ph, nw: (b, h, 0, 0, 0)),
            pl.BlockSpec((None, None, NSA_GROUP, NSA_HD), lambda b, h, k, ph, nw: (b, h, 0, 0)),
        ],
        out_specs=pl.BlockSpec((None, None, NSA_GROUP, NSA_HD), lambda b, h, k, ph, nw: (b, h, 0, 0)),
        scratch_shapes=[
            pltpu.VMEM((NSA_GROUP, 1), F32),
            pltpu.VMEM((NSA_GROUP, 1), F32),
            pltpu.VMEM((NSA_GROUP, NSA_HD), F32),
        ],
    )
    return pl.pallas_call(
        functools.partial(_nsa_decode_attend_body, n_sel=n_sel, kvh=kvh),
        grid_spec=grid_spec,
        out_shape=jax.ShapeDtypeStruct((batch, kvh, NSA_GROUP, NSA_HD), F32),
        compiler_params=_cparams(("parallel", "parallel", "arbitrary")),
        name="nsa_decode_attend",
    )(phys, isnew, q, pool, snew, win, wnew, gate, o_c)


def kernel(x_prompt, x_sample, cache_cmp_kv, cache_sel_kv, state_win_kv, state_hgrn, page_table,
           w_in, w_out, w_up, w_down, cmp_pe, cmp_w, lb_logits, hgrn_norm,
           norm_pre_mix, norm_post_mix, norm_pre_ffn, norm_post_ffn):
    depth = w_in.shape[0]
    bp, tp, d_model = x_prompt.shape
    bs, ts, _ = x_sample.shape
    assert ts == 1, "decode path handles one new token per sequence"
    d_hgrn = hgrn_norm.shape[1]
    heads = d_hgrn // HGRN_DK
    d_nsa = w_out.shape[1] - d_hgrn
    d_ff = w_up.shape[2]
    kvh = d_nsa // NSA_HD // NSA_GROUP
    kv_w = kvh * 2 * NSA_HD
    n_main = 4 * d_hgrn + d_nsa + 3 * kv_w
    assert w_in.shape[2] == n_main + 3 * kvh * NSA_GROUP
    col_q = 4 * d_hgrn
    col_ckv = col_q + d_nsa
    col_skv = col_ckv + kv_w
    col_wkv = col_skv + kv_w
    past_len = page_table.shape[1] * PAGE_SIZE
    wb = state_win_kv.shape[2]
    assert wb == WINDOW and past_len % BLOCK == 0
    nb_p = tp // BLOCK
    nb_past = past_len // BLOCK
    n_blk_s = nb_past + 1

    p_lb = jax.nn.softmax(lb_logits.astype(F32), axis=0)
    lbs = jnp.cumsum(p_lb, axis=0) - p_lb[0:1]

    def gate_weights(w):
        gate = w[:, n_main:].reshape(d_model, 3, kvh, NSA_GROUP).transpose(0, 2, 1, 3)
        gate = gate.reshape(d_model, kvh, 3 * NSA_GROUP)
        return jnp.pad(gate, ((0, 0), (0, 0), (0, LANES - 3 * NSA_GROUP))).reshape(1, d_model, kvh * LANES)

    def lb_rows(lb):
        lb = lb.reshape(heads, 1, HGRN_DK)
        rows = jnp.concatenate([jnp.log(jnp.maximum(lb, LB_FLOOR)), jnp.log1p(-lb), 1.0 - lb], axis=1)
        return jnp.pad(rows, ((0, 0), (0, 5), (0, 0)))

    kc_len = min(tp, KEY_CHUNK)
    nbp = -(-nb_p // LANES) * LANES
    key_blk = (jnp.arange(tp) // BLOCK).reshape(tp // kc_len, 1, kc_len)
    expand = (jnp.arange(nbp)[None, :, None] == key_blk).astype(BF16)
    pool_cmp = cache_cmp_kv.reshape(depth, cache_cmp_kv.shape[1], PAGE_SIZE * kvh * 2, NSA_HD)
    pool_sel = cache_sel_kv.reshape(depth, cache_sel_kv.shape[1] * (PAGE_SIZE // BLOCK), BLOCK * kvh * 2, NSA_HD)
    win_state = state_win_kv.reshape(depth, bs, wb * kvh * 2, NSA_HD)

    xp = x_prompt.reshape(bp * tp, d_model)
    xs = x_sample.reshape(bs, d_model)
    hp = _rmsnorm_bf16(xp, norm_pre_mix[0])
    hs = _rmsnorm_bf16(xs, norm_pre_mix[0])
    outs = {k: [] for k in ("p_cmp", "p_sel", "p_win", "p_rec", "s_cmp", "s_sel", "s_win", "s_rec")}
    kv5 = lambda a, b, t: a.reshape(b, t, kvh, 2, NSA_HD)

    def ffn_tail(mix, x_res, l):
        y = _matmul(mix, w_out, l, d_model)
        hmid, f_in = _norm_res(y, norm_post_mix[l], x_res, norm_pre_ffn[l])
        f = _matmul(f_in, w_up, l, d_ff, act=True, out_dtype=BF16)
        y = _matmul(f, w_down, l, d_model)
        if l + 1 < depth:
            return _norm_res(y, norm_post_ffn[l], hmid, norm_pre_mix[l + 1])
        return _norm_res(y, norm_post_ffn[l], hmid), None

    for l in range(depth):
        w_gate = gate_weights(w_in[l])
        w_c = cmp_w[l].reshape(2, BLOCK * NSA_HD, NSA_HD).astype(BF16)
        pe_c = cmp_pe[l].transpose(1, 0, 2)
        lbp = lb_rows(lbs[l])
        gn = hgrn_norm[l].reshape(heads, 1, HGRN_DK)

        u = _matmul(hp, w_in, l, n_main)
        ug = _matmul(hp, w_gate, 0, kvh * LANES)
        o_r, st = _hgrn_prompt(u, lbp, gn, batch=bp, seq=tp, heads=heads, tt=min(tp, 256))
        ckv = u[:, col_ckv:col_skv]
        skv = u[:, col_skv:col_wkv]
        wkv = u[:, col_wkv:n_main]
        xc = ckv.reshape(bp, nb_p, BLOCK, kvh, 2, NSA_HD).transpose(4, 0, 3, 1, 2, 5)
        kc = _compress(xc.reshape(2, bp * kvh * nb_p, BLOCK * NSA_HD),
                       pe_c.reshape(2, 1, BLOCK * NSA_HD), w_c)
        kc = kc.reshape(2, bp, kvh, nb_p, NSA_HD).transpose(1, 2, 0, 3, 4)
        kc = jnp.pad(kc, ((0, 0), (0, 0), (0, 0), (0, nbp - nb_p), (0, 0))).astype(BF16)
        o_b = _nsa_prompt(u, ug, kc, expand, batch=bp, seq=tp, kvh=kvh,
                          col_q=col_q, col_skv=col_skv, col_wkv=col_wkv)
        xp, hp = ffn_tail(jnp.concatenate([o_r, o_b], axis=1), xp, l)
        outs["p_cmp"].append(kv5(ckv, bp, tp))
        outs["p_sel"].append(kv5(skv, bp, tp))
        outs["p_win"].append(kv5(wkv, bp, tp)[:, tp - min(WINDOW, tp):])
        outs["p_rec"].append(st)

        us = _matmul(hs, w_in, l, n_main)
        ugs = _matmul(hs, w_gate, 0, kvh * LANES)
        o_r, st = _hgrn_decode(us, lbp, gn, state_hgrn[l], batch=bs, heads=heads)
        q_s = us[:, col_q:col_ckv].reshape(bs, kvh, NSA_GROUP, NSA_HD)
        ckv = us[:, col_ckv:col_skv]
        skv = us[:, col_skv:col_wkv]
        wkv = us[:, col_wkv:n_main]
        kc_s = _compress_paged(pool_cmp, page_table, pe_c, w_c, layer=l, kvh=kvh)
        o_c, idx = _nsa_decode_select(q_s, kc_s, past_len=past_len, n_blk=n_blk_s)
        idx = idx[..., 0]
        ppos = jnp.minimum(idx, nb_past - 1) * BLOCK
        page = jnp.take_along_axis(page_table, (ppos // PAGE_SIZE).reshape(bs, -1), axis=1)
        phys = page.reshape(idx.shape) * (PAGE_SIZE // BLOCK) + (ppos % PAGE_SIZE) // BLOCK
        isnew = (idx >= nb_past).astype(jnp.int32)
        gate_s = ugs.reshape(bs, kvh, LANES)[:, :, :3 * NSA_GROUP]
        gate_s = jnp.broadcast_to(gate_s.reshape(bs, kvh, 3, NSA_GROUP, 1), (bs, kvh, 3, NSA_GROUP, NSA_HD))
        o_b = _nsa_decode_attend(phys.astype(jnp.int32), isnew, q_s, pool_sel,
                                 skv.reshape(bs, kvh, 1, 2 * NSA_HD), win_state,
                                 wkv.reshape(bs, kvh, 1, 2 * NSA_HD), gate_s, o_c, layer=l)
        mix = jnp.concatenate([o_r, o_b.reshape(bs, d_nsa)], axis=1).astype(BF16)
        xs, hs = ffn_tail(mix, xs, l)
        outs["s_cmp"].append(kv5(ckv, bs, ts))
        outs["s_sel"].append(kv5(skv, bs, ts))
        outs["s_win"].append(jnp.concatenate([state_win_kv[l][:, ts:], kv5(wkv, bs, ts)], axis=1))
        outs["s_rec"].append(st)

    stack = lambda k: jnp.stack(outs[k])
    return (xp.reshape(bp, tp, d_model), xs.reshape(bs, ts, d_model),
            stack("p_cmp"), stack("p_sel"), stack("p_win"), stack("p_rec"),
            stack("s_cmp"), stack("s_sel"), stack("s_win"), stack("s_rec"))
```

```python
import functools
import math

import jax
import jax.numpy as jnp
from jax import lax
from jax.experimental import pallas as pl
from jax.experimental.pallas import tpu as pltpu

F32 = jnp.float32
BF16 = jnp.bfloat16

HGRN_DK = 128
NSA_HD = 128
NSA_GROUP = 4
BLOCK = 64
N_SELECT = 16
WINDOW = 512
PAGE_SIZE = 128
EPS = 1e-6
NEG_INF = -1e30
LB_FLOOR = 1e-30
ATTN_SCALE = NSA_HD ** -0.5
FORCE_SCORE = NSA_GROUP + 1.0

LANES = 128
VMEM_LIMIT = 56 * 1024 * 1024

HGRN_CHUNK = 64
HGRN_SUB = 16
HGRN_HEADS_PER_STEP = 4
PAGES_PER_STEP = 16
KEY_CHUNK = 512


def _cparams(sem):
    return pltpu.CompilerParams(dimension_semantics=sem, vmem_limit_bytes=VMEM_LIMIT)


def _dot(a, b):
    return jnp.dot(a, b, preferred_element_type=F32)


def _dot_nt(a, b):
    return lax.dot_general(a, b, (((1,), (1,)), ((), ())), preferred_element_type=F32)


def _dot_tn(a, b):
    return lax.dot_general(a, b, (((0,), (0,)), ((), ())), preferred_element_type=F32)


def _sigmoid(x):
    return 1.0 / (1.0 + jnp.exp(-x))


def _silu(x):
    return x * _sigmoid(x)


def _rms(x, g):
    ms = jnp.mean(x * x, axis=-1, keepdims=True)
    return x * lax.rsqrt(ms + EPS) * g


def _masked_softmax(s, mask):
    s = jnp.where(mask, s, NEG_INF)
    e = jnp.where(mask, jnp.exp(s - jnp.max(s, axis=-1, keepdims=True)), 0.0)
    return e / jnp.maximum(jnp.sum(e, axis=-1, keepdims=True), 1e-30)


def _rmsnorm_body(x_ref, g_ref, o_ref):
    o_ref[...] = _rms(x_ref[...], g_ref[...]).astype(o_ref.dtype)


def _rmsnorm_bf16(x, g):
    m, d = x.shape
    tm = min(m, 256)
    return pl.pallas_call(
        _rmsnorm_body,
        grid=(m // tm,),
        in_specs=[pl.BlockSpec((tm, d), lambda i: (i, 0)), pl.BlockSpec((1, d), lambda i: (0, 0))],
        out_specs=pl.BlockSpec((tm, d), lambda i: (i, 0)),
        out_shape=jax.ShapeDtypeStruct((m, d), BF16),
        compiler_params=_cparams(("parallel",)),
        name="rmsnorm",
    )(x, g.reshape(1, d))


def _norm_res_body(y_ref, g1_ref, r_ref, *rest):
    out = r_ref[...] + _rms(y_ref[...], g1_ref[...])
    if len(rest) == 1:
        rest[0][...] = out
    else:
        g2_ref, o_ref, n_ref = rest
        o_ref[...] = out
        n_ref[...] = _rms(out, g2_ref[...]).astype(n_ref.dtype)


def _norm_res(y, g1, res, g2=None):
    m, d = y.shape
    tm = min(m, 256)
    row = pl.BlockSpec((tm, d), lambda i: (i, 0))
    vec = pl.BlockSpec((1, d), lambda i: (0, 0))
    if g2 is None:
        return pl.pallas_call(
            _norm_res_body, grid=(m // tm,), in_specs=[row, vec, row], out_specs=row,
            out_shape=jax.ShapeDtypeStruct((m, d), F32),
            compiler_params=_cparams(("parallel",)), name="norm_res",
        )(y, g1.reshape(1, d), res)
    return pl.pallas_call(
        _norm_res_body, grid=(m // tm,), in_specs=[row, vec, row, vec], out_specs=[row, row],
        out_shape=[jax.ShapeDtypeStruct((m, d), F32), jax.ShapeDtypeStruct((m, d), BF16)],
        compiler_params=_cparams(("parallel",)), name="norm_res_norm",
    )(y, g1.reshape(1, d), res, g2.reshape(1, d))


def _matmul_body(x_ref, xs_ref, w_ref, o_ref, os_ref, *wb_ref, act, nk, w_t):
    i, kk = pl.program_id(1), pl.program_id(2)
    mm = _dot_nt if w_t else _dot

    def finish(part, ref, first):
        if nk == 1:
            if act:
                part = jnp.square(jnp.maximum(part, 0.0))
            ref[...] = part.astype(ref.dtype)
        else:
            @pl.when(first)
            def _():
                ref[...] = part

            @pl.when(jnp.logical_not(first))
            def _():
                ref[...] += part

    if nk == 1:
        @pl.when(i == 0)
        def _():
            wb_ref[0][...] = w_ref[...].astype(BF16)
        wb = wb_ref[0][...]
    else:
        wb = w_ref[...].astype(BF16)

    finish(mm(x_ref[...], wb), o_ref, kk == 0)

    @pl.when(i == 0)
    def _():
        finish(mm(xs_ref[...], wb), os_ref, kk == 0)


def _matmul(x, xs, w, layer, n_out, *, act=False, out_dtype=F32, w_t=False):
    m, k = x.shape
    ms = xs.shape[0]
    tm = min(m, 1024)
    tk, tn = (k, 512) if k <= 4096 else (2048, 1024)
    tn = math.gcd(tn, n_out)
    nk = k // tk
    assert m % tm == 0 and n_out % tn == 0 and k % tk == 0
    assert nk == 1 or (out_dtype == F32 and not act)
    if w_t:
        w_spec = pl.BlockSpec((None, tn, tk), lambda j, i, kk: (layer, j, kk))
        wb_shape = (tn, tk)
    else:
        w_spec = pl.BlockSpec((None, tk, tn), lambda j, i, kk: (layer, kk, j))
        wb_shape = (tk, tn)
    return pl.pallas_call(
        functools.partial(_matmul_body, act=act, nk=nk, w_t=w_t),
        grid=(n_out // tn, m // tm, nk),
        in_specs=[
            pl.BlockSpec((tm, tk), lambda j, i, kk: (i, kk)),
            pl.BlockSpec((ms, tk), lambda j, i, kk: (0, kk)),
            w_spec,
        ],
        out_specs=[
            pl.BlockSpec((tm, tn), lambda j, i, kk: (i, j)),
            pl.BlockSpec((ms, tn), lambda j, i, kk: (0, j)),
        ],
        out_shape=[
            jax.ShapeDtypeStruct((m, n_out), out_dtype),
            jax.ShapeDtypeStruct((ms, n_out), out_dtype),
        ],
        scratch_shapes=[pltpu.VMEM(wb_shape, BF16)] if nk == 1 else [],
        compiler_params=_cparams(("arbitrary", "arbitrary", "arbitrary")),
        name="matmul",
    )(x, xs, w)


def _hgrn_gates(z, lbp):
    a = lbp[0:1, :]
    log_sig = jnp.minimum(z, 0.0) - jnp.log1p(jnp.exp(-jnp.abs(z)))
    b = lbp[1:2, :] + log_sig
    logf = jnp.maximum(a, b) + jnp.log1p(jnp.exp(-jnp.abs(a - b)))
    k = lbp[2:3, :] * _sigmoid(-z)
    return logf, k


def _hgrn_out(o, gn, hg):
    return _rms(o, gn) * _silu(hg)


def _hgrn_chunk(q, z, v, hg, lbp, gn, st, consts):
    c, sub = HGRN_CHUNK, HGRN_SUB
    ns = c // sub
    tri, ones, t_in_sub = consts
    q = _silu(q)
    logf, k = _hgrn_gates(z, lbp)

    hi = logf.astype(BF16)
    r1 = logf - hi.astype(F32)
    mid = r1.astype(BF16)
    lo = (r1 - mid.astype(F32)).astype(BF16)
    cb = _dot(tri, hi) + _dot(tri, mid) + _dot(tri, lo)

    q3 = q.reshape(ns, sub, HGRN_DK)
    k3 = k.reshape(ns, sub, HGRN_DK)
    v3 = v.reshape(ns, sub, HGRN_DK)
    cb3 = cb.reshape(ns, sub, HGRN_DK)
    e_end = cb3[:, sub - 1:sub, :]
    b_beg = jnp.concatenate([jnp.zeros((1, 1, HGRN_DK), F32), e_end[:ns - 1]], axis=0)
    qt = q3 * jnp.exp(cb3 - b_beg)
    kt = k3 * jnp.exp(e_end - cb3)

    q_cols, k_cols = [], []
    zeros_sub = jnp.zeros((sub, HGRN_DK), F32)
    for j in range(ns - 1):
        qc, kc = [], []
        for i in range(ns):
            qc.append(qt[i] * jnp.exp(b_beg[i] - e_end[j]) if i > j else zeros_sub)
            kc.append(kt[i] if i == j else zeros_sub)
        q_cols.append(jnp.concatenate(qc, axis=0))
        k_cols.append(jnp.concatenate(kc, axis=0))
    q_hat = jnp.concatenate(q_cols, axis=1).astype(BF16)
    k_hat = jnp.concatenate(k_cols, axis=1).astype(BF16)
    att_off = _dot_nt(q_hat, k_hat)
    o = _dot(att_off.astype(BF16), v.astype(BF16))

    parts = []
    for s in range(sub):
        dec = jnp.exp(cb3 - cb3[:, s:s + 1, :])
        p = jnp.where(t_in_sub >= s, q3 * k3[:, s:s + 1, :] * dec, 0.0)
        parts.append(p.reshape(c, HGRN_DK).astype(BF16))
    rsum = _dot(jnp.concatenate(parts, axis=0), ones)
    o_diag = jnp.zeros((ns, sub, HGRN_DK), F32)
    for s in range(sub):
        o_diag = o_diag + rsum[s * c:(s + 1) * c, :].reshape(ns, sub, HGRN_DK) * v3[:, s:s + 1, :]
    o = o + o_diag.reshape(c, HGRN_DK)

    o = o + _dot_nt((q * jnp.exp(cb)).astype(BF16), st.astype(BF16))
    last = cb[c - 1:c, :]
    kd = (k * jnp.exp(last - cb)).astype(BF16)
    st_new = st * jnp.exp(last) + _dot_tn(v.astype(BF16), kd)
    return _hgrn_out(o, gn, hg), st_new


def _hgrn_prompt_body(q_ref, f_ref, i_ref, g_ref, lb_ref, gn_ref, o_ref, st_ref, s_ref, *, n_chunks):
    c, sub = HGRN_CHUNK, HGRN_SUB
    hp = s_ref.shape[0]
    tstep = pl.program_id(2)

    @pl.when(tstep == 0)
    def _():
        s_ref[...] = jnp.zeros_like(s_ref)

    row = lax.broadcasted_iota(jnp.int32, (c, c), 0)
    col = lax.broadcasted_iota(jnp.int32, (c, c), 1)
    consts = (
        jnp.where(row >= col, 1.0, 0.0).astype(BF16),
        jnp.ones((LANES, LANES), BF16),
        lax.broadcasted_iota(jnp.int32, (c // sub, sub, HGRN_DK), 1),
    )

    def chunk(ci, carry):
        rows = pl.ds(pl.multiple_of(ci * c, c), c)
        for hh in range(hp):
            cols = slice(hh * HGRN_DK, (hh + 1) * HGRN_DK)
            out, st_new = _hgrn_chunk(q_ref[rows, cols], f_ref[rows, cols], i_ref[rows, cols],
                                      g_ref[rows, cols], lb_ref[hh], gn_ref[hh], s_ref[hh], consts)
            s_ref[hh] = st_new
            o_ref[rows, cols] = out.astype(o_ref.dtype)
        return carry

    lax.fori_loop(0, n_chunks, chunk, 0)

    @pl.when(tstep == pl.num_programs(2) - 1)
    def _():
        for hh in range(hp):
            st_ref[hh] = s_ref[hh].T


def _hgrn_prompt(u, lbp, gn, *, batch, seq, heads, tt):
    nt = seq // tt
    hp = math.gcd(heads, HGRN_HEADS_PER_STEP)
    w = hp * HGRN_DK
    ng = heads // hp
    blk = lambda part: pl.BlockSpec((tt, w), lambda b, h, t: (b * nt + t, part * ng + h))
    return pl.pallas_call(
        functools.partial(_hgrn_prompt_body, n_chunks=tt // HGRN_CHUNK),
        grid=(batch, ng, nt),
        in_specs=[
            blk(0), blk(1), blk(2), blk(3),
            pl.BlockSpec((hp, 8, HGRN_DK), lambda b, h, t: (h, 0, 0)),
            pl.BlockSpec((hp, 1, HGRN_DK), lambda b, h, t: (h, 0, 0)),
        ],
        out_specs=[
            pl.BlockSpec((tt, w), lambda b, h, t: (b * nt + t, h)),
            pl.BlockSpec((None, hp, HGRN_DK, HGRN_DK), lambda b, h, t: (b, h, 0, 0)),
        ],
        out_shape=[
            jax.ShapeDtypeStruct((batch * seq, heads * HGRN_DK), BF16),
            jax.ShapeDtypeStruct((batch, heads, HGRN_DK, HGRN_DK), F32),
        ],
        scratch_shapes=[pltpu.VMEM((hp, HGRN_DK, HGRN_DK), F32)],
        compiler_params=_cparams(("parallel", "parallel", "arbitrary")),
        name="hgrn_prompt",
    )(u, u, u, u, lbp, gn)


def _hgrn_decode_body(q_ref, f_ref, i_ref, g_ref, lb_ref, gn_ref, s0_ref, o_ref, st_ref):
    r = pl.ds(pl.program_id(1), 1)
    logf, k = _hgrn_gates(f_ref[r, :], lb_ref[...])
    q = _silu(q_ref[r, :])
    v = i_ref[r, :]

    def col(x):
        return jnp.broadcast_to(x, (HGRN_DK, HGRN_DK)).T

    s_new = col(jnp.exp(logf)) * s0_ref[...] + col(k) * v
    st_ref[...] = s_new
    o = jnp.sum(col(q) * s_new, axis=0, keepdims=True)
    o_ref[r, :] = _hgrn_out(o, gn_ref[...], g_ref[r, :])


def _hgrn_decode(u, lbp, gn, s0, *, batch, heads):
    blk = lambda off: pl.BlockSpec((batch, HGRN_DK), lambda h, b: (0, off + h))
    return pl.pallas_call(
        _hgrn_decode_body,
        grid=(heads, batch),
        in_specs=[
            blk(0), blk(heads), blk(2 * heads), blk(3 * heads),
            pl.BlockSpec((None, 8, HGRN_DK), lambda h, b: (h, 0, 0)),
            pl.BlockSpec((None, 1, HGRN_DK), lambda h, b: (h, 0, 0)),
            pl.BlockSpec((None, None, HGRN_DK, HGRN_DK), lambda h, b: (b, h, 0, 0)),
        ],
        out_specs=[
            pl.BlockSpec((batch, HGRN_DK), lambda h, b: (0, h)),
            pl.BlockSpec((None, None, HGRN_DK, HGRN_DK), lambda h, b: (b, h, 0, 0)),
        ],
        out_shape=[
            jax.ShapeDtypeStruct((batch, heads * HGRN_DK), F32),
            jax.ShapeDtypeStruct(s0.shape, F32),
        ],
        compiler_params=_cparams(("parallel", "arbitrary")),
        name="hgrn_decode",
    )(u, u, u, u, lbp, gn, s0)


def _compress_body(x_ref, pe_ref, w_ref, o_ref):
    o_ref[...] = _dot((x_ref[...] + pe_ref[...]).astype(BF16), w_ref[...])


def _compress(x, pe, w):
    _, r, kdim = x.shape
    tm = min(r, 256)
    return pl.pallas_call(
        _compress_body,
        grid=(2, r // tm),
        in_specs=[
            pl.BlockSpec((None, tm, kdim), lambda c, i: (c, i, 0)),
            pl.BlockSpec((None, 1, kdim), lambda c, i: (c, 0, 0)),
            pl.BlockSpec((None, kdim, NSA_HD), lambda c, i: (c, 0, 0)),
        ],
        out_specs=pl.BlockSpec((None, tm, NSA_HD), lambda c, i: (c, i, 0)),
        out_shape=jax.ShapeDtypeStruct((2, r, NSA_HD), F32),
        compiler_params=_cparams(("parallel", "parallel")),
        name="compress",
    )(x, pe, w)


def _compress_paged_body(pt_ref, *refs, kvh):
    del pt_ref
    npg = PAGES_PER_STEP
    page_refs = refs[:npg]
    pe_ref, w_ref, o_ref, buf_ref, lhs_ref = refs[npg:]
    rpp = PAGE_SIZE * kvh * 2
    for p in range(npg):
        buf_ref[p * rpp:(p + 1) * rpp, :] = page_refs[p][...]
    nblk = npg * PAGE_SIZE // BLOCK
    blk_stride = BLOCK * kvh * 2
    for c in range(2):
        for r in range(BLOCK):
            pieces = [buf_ref[pl.ds(r * kvh * 2 + 2 * h + c, nblk, stride=blk_stride), :] for h in range(kvh)]
            x = jnp.concatenate(pieces, axis=0) + pe_ref[c, r:r + 1, :]
            lhs_ref[c, :, r * NSA_HD:(r + 1) * NSA_HD] = x.astype(BF16)
        o_ref[c] = _dot(lhs_ref[c], w_ref[c]).reshape(kvh, nblk, NSA_HD)


def _compress_paged(pool, page_table, pe, w, *, layer, kvh):
    batch, n_pages = page_table.shape
    npg = PAGES_PER_STEP
    rpp = pool.shape[2]
    nblk = npg * PAGE_SIZE // BLOCK

    def page_spec(p):
        return pl.BlockSpec((None, None, rpp, NSA_HD),
                            lambda b, g, pt: (layer, pt[b, g * npg + p], 0, 0))

    grid_spec = pltpu.PrefetchScalarGridSpec(
        num_scalar_prefetch=1,
        grid=(batch, n_pages // npg),
        in_specs=[page_spec(p) for p in range(npg)] + [
            pl.BlockSpec((2, BLOCK, NSA_HD), lambda b, g, pt: (0, 0, 0)),
            pl.BlockSpec((2, BLOCK * NSA_HD, NSA_HD), lambda b, g, pt: (0, 0, 0)),
        ],
        out_specs=pl.BlockSpec((2, None, kvh, nblk, NSA_HD), lambda b, g, pt: (0, b, 0, g, 0)),
        scratch_shapes=[
            pltpu.VMEM((npg * rpp, NSA_HD), F32),
            pltpu.VMEM((2, kvh * nblk, BLOCK * NSA_HD), BF16),
        ],
    )
    return pl.pallas_call(
        functools.partial(_compress_paged_body, kvh=kvh),
        grid_spec=grid_spec,
        out_shape=jax.ShapeDtypeStruct((2, batch, kvh, n_pages * PAGE_SIZE // BLOCK, NSA_HD), F32),
        compiler_params=_cparams(("parallel", "arbitrary")),
        name="compress_paged",
    )(page_table, *([pool] * npg), pe, w)


def _nsa_prompt_body(q_ref, kc_ref, vct_ref, skv_ref, wkv_ref, gate_ref, e_ref, o_ref,
                     ks_ref, vst_ref, kw_ref, vwt_ref, m_ref, l_ref, acc_ref, *, tq, seq, n_blk, span, kc_len):
    i = pl.program_id(2)

    @pl.when(i == 0)
    def _():
        ks_ref[...] = skv_ref[:, :NSA_HD].astype(BF16)
        kw_ref[...] = wkv_ref[:, :NSA_HD].astype(BF16)
        for c in range(seq // kc_len):
            vst_ref[c] = skv_ref[c * kc_len:(c + 1) * kc_len, NSA_HD:].T.astype(BF16)
        for c in range(seq // tq):
            vwt_ref[c] = wkv_ref[c * tq:(c + 1) * tq, NSA_HD:].T.astype(BF16)

    nbp = kc_ref.shape[0]
    nb8 = -(-n_blk // 8) * 8
    n_sel = min(N_SELECT, n_blk)
    q0 = i * tq
    qpos = q0 + lax.broadcasted_iota(jnp.int32, (1, tq), 1)
    cur = qpos // BLOCK
    n_sub = lax.broadcasted_iota(jnp.int32, (nbp, 1), 0)
    cmask = (n_sub * BLOCK + (BLOCK - 1) <= qpos) & (n_sub < seq // BLOCK)

    kc = kc_ref[...]
    vct = vct_ref[...]
    q_all = jnp.concatenate(
        [(q_ref[:, g * NSA_HD:(g + 1) * NSA_HD] * ATTN_SCALE).astype(BF16) for g in range(NSA_GROUP)], axis=0)
    per_head = lambda a: jnp.concatenate([a] * NSA_GROUP, axis=1)
    head = lambda a, g: a[:, g * tq:(g + 1) * tq]

    cmask4 = per_head(cmask)
    s = jnp.where(cmask4, _dot_nt(kc, q_all), NEG_INF)
    e = jnp.where(cmask4, jnp.exp(s - jnp.max(s, axis=0, keepdims=True)), 0.0)
    p = e / jnp.maximum(jnp.sum(e, axis=0, keepdims=True), 1e-30)
    o_c = _dot(vct, p.astype(BF16))
    imp = head(p, 0)
    for g in range(1, NSA_GROUP):
        imp = imp + head(p, g)

    forced = (n_sub == 0) | (n_sub == cur) | (n_sub == cur - 1)
    score = jnp.where(n_sub <= cur, jnp.where(forced, FORCE_SCORE, imp), -1.0)[:nb8, :]
    j_idx = lax.broadcasted_iota(jnp.int32, (nb8, tq), 0)
    score = jnp.where(j_idx < n_blk, score, -2.0)
    cnt = jnp.zeros((nb8, tq), F32)
    for j in range(n_blk):
        rj = score[j:j + 1, :]
        beats = jnp.where(rj > score, 1.0, jnp.where((rj == score) & (j_idx > j), 1.0, 0.0))
        cnt = cnt + beats
    sel = jnp.where((cnt < n_sel) & (j_idx <= cur) & (j_idx < n_blk), 1.0, 0.0)
    if nbp > nb8:
        sel = jnp.concatenate([sel, jnp.zeros((nbp - nb8, tq), F32)], axis=0)
    sel = sel.astype(BF16)

    m_ref[...] = jnp.full_like(m_ref, NEG_INF)
    l_ref[...] = jnp.zeros_like(l_ref)
    acc_ref[...] = jnp.zeros_like(acc_ref)

    def key_chunk(ci, carry):
        k0 = pl.multiple_of(ci * kc_len, kc_len)
        kk = ks_ref[pl.ds(k0, kc_len), :]
        vt = vst_ref[ci]
        kpos = k0 + lax.broadcasted_iota(jnp.int32, (kc_len, 1), 0)
        bias = jnp.where((_dot(e_ref[ci], sel) > 0.5) & (kpos <= qpos), 0.0, NEG_INF)
        s = _dot_nt(kk, q_all) + per_head(bias)
        m_old = m_ref[...]
        m_new = jnp.maximum(m_old, jnp.max(s, axis=0, keepdims=True))
        alpha = jnp.exp(m_old - m_new)
        p = jnp.exp(s - m_new)
        l_ref[...] = alpha * l_ref[...] + jnp.sum(p, axis=0, keepdims=True)
        acc_ref[...] = alpha * acc_ref[...] + _dot(vt, p.astype(BF16))
        m_ref[...] = m_new
        return carry

    lax.fori_loop(0, (q0 + tq + kc_len - 1) // kc_len, key_chunk, 0)

    nwb = span // tq
    blk0 = jnp.maximum(i + 1 - nwb, 0)
    start = pl.multiple_of(blk0 * tq, tq)
    wpos = start + lax.broadcasted_iota(jnp.int32, (span, 1), 0)
    dist = qpos - wpos
    wbias = jnp.where((dist >= 0) & (dist < WINDOW), 0.0, NEG_INF)
    kw = kw_ref[pl.ds(start, span), :]
    vwt = jnp.concatenate([vwt_ref[blk0 + j] for j in range(nwb)], axis=1)

    o_s = acc_ref[...] / l_ref[...]
    s = _dot_nt(kw, q_all) + per_head(wbias)
    e = jnp.exp(s - jnp.max(s, axis=0, keepdims=True))
    o_w = _dot(vwt, e.astype(BF16)) / jnp.sum(e, axis=0, keepdims=True)

    gates = _sigmoid(gate_ref[...]).T
    for g in range(NSA_GROUP):
        gc = gates[g:g + 1, :]
        gs = gates[NSA_GROUP + g:NSA_GROUP + g + 1, :]
        gw = gates[2 * NSA_GROUP + g:2 * NSA_GROUP + g + 1, :]
        o = gc * head(o_c, g) + gs * head(o_s, g) + gw * head(o_w, g)
        o_ref[:, g * NSA_HD:(g + 1) * NSA_HD] = o.T.astype(o_ref.dtype)


def _nsa_prompt(u, ug, kc, vct, expand, *, batch, seq, kvh, col_q, col_skv, col_wkv):
    tq = min(seq, LANES)
    assert tq == LANES or seq < LANES
    nq = seq // tq
    span = min(seq, WINDOW + tq)
    qw = NSA_GROUP * NSA_HD
    nbp = kc.shape[2]
    n_kc, kc_len, _ = expand.shape
    return pl.pallas_call(
        functools.partial(_nsa_prompt_body, tq=tq, seq=seq, n_blk=seq // BLOCK, span=span, kc_len=kc_len),
        grid=(batch, kvh, nq),
        in_specs=[
            pl.BlockSpec((tq, qw), lambda b, h, i: (b * nq + i, col_q // qw + h)),
            pl.BlockSpec((None, None, nbp, NSA_HD), lambda b, h, i: (b, h, 0, 0)),
            pl.BlockSpec((None, None, NSA_HD, nbp), lambda b, h, i: (b, h, 0, 0)),
            pl.BlockSpec((seq, 2 * NSA_HD), lambda b, h, i: (b, col_skv // (2 * NSA_HD) + h)),
            pl.BlockSpec((seq, 2 * NSA_HD), lambda b, h, i: (b, col_wkv // (2 * NSA_HD) + h)),
            pl.BlockSpec((tq, LANES), lambda b, h, i: (b * nq + i, h)),
            pl.BlockSpec((n_kc, kc_len, nbp), lambda b, h, i: (0, 0, 0)),
        ],
        out_specs=pl.BlockSpec((tq, qw), lambda b, h, i: (b * nq + i, h)),
        out_shape=jax.ShapeDtypeStruct((batch * seq, kvh * qw), BF16),
        scratch_shapes=[
            pltpu.VMEM((seq, NSA_HD), BF16),
            pltpu.VMEM((n_kc, NSA_HD, kc_len), BF16),
            pltpu.VMEM((seq, NSA_HD), BF16),
            pltpu.VMEM((nq, NSA_HD, tq), BF16),
            pltpu.VMEM((1, NSA_GROUP * tq), F32),
            pltpu.VMEM((1, NSA_GROUP * tq), F32),
            pltpu.VMEM((NSA_HD, NSA_GROUP * tq), F32),
        ],
        compiler_params=_cparams(("parallel", "parallel", "arbitrary")),
        name="nsa_prompt",
    )(u, kc, vct, u, u, ug, expand)


def _nsa_decode_select_body(q_ref, kc_ref, oc_ref, idx_ref, *, past_len, n_blk):
    nbc = kc_ref.shape[1]
    npad = -(-n_blk // LANES) * LANES
    n_sel = min(N_SELECT, n_blk)
    qpos = past_len
    cur = qpos // BLOCK
    q = q_ref[...].astype(BF16)
    n_lane = lax.broadcasted_iota(jnp.int32, (1, nbc), 1)
    cmask = n_lane * BLOCK + (BLOCK - 1) <= qpos
    p = _masked_softmax(_dot_nt(q, kc_ref[0].astype(BF16)) * ATTN_SCALE, cmask)
    oc_ref[...] = _dot(p.astype(BF16), kc_ref[1].astype(BF16))
    imp = jnp.sum(p, axis=0, keepdims=True)
    imp = jnp.concatenate([imp, jnp.zeros((1, npad - nbc), F32)], axis=1)

    j_lane = lax.broadcasted_iota(jnp.int32, (1, npad), 1)
    forced = (j_lane == 0) | (j_lane == cur) | (j_lane == cur - 1)
    score = jnp.where(j_lane <= cur, jnp.where(forced, FORCE_SCORE, imp), -1.0)
    score = jnp.where(j_lane < n_blk, score, -2.0)
    s_row = jnp.broadcast_to(score, (npad, npad))
    s_col = s_row.T
    jp = lax.broadcasted_iota(jnp.int32, (npad, npad), 0)
    jj = lax.broadcasted_iota(jnp.int32, (npad, npad), 1)
    beats = jnp.where(s_col > s_row, 1.0, jnp.where((s_col == s_row) & (jp < jj), 1.0, 0.0))
    rank = jnp.sum(beats, axis=0, keepdims=True)
    k_sub = lax.broadcasted_iota(jnp.int32, (N_SELECT, npad), 0).astype(F32)
    j_f = lax.broadcasted_iota(jnp.int32, (N_SELECT, npad), 1).astype(F32)
    hit = jnp.where((rank == k_sub) & (j_f < n_blk), j_f, 0.0)
    idx = jnp.sum(hit, axis=1, keepdims=True)
    idx = jnp.where(k_sub[:, :1] < n_sel, idx, 0.0)
    idx_ref[...] = jnp.broadcast_to(idx, (N_SELECT, LANES)).astype(jnp.int32)


def _nsa_decode_select(q, kc, *, past_len, n_blk):
    batch, kvh = q.shape[:2]
    nbc = kc.shape[3]
    return pl.pallas_call(
        functools.partial(_nsa_decode_select_body, past_len=past_len, n_blk=n_blk),
        grid=(batch, kvh),
        in_specs=[
            pl.BlockSpec((None, None, NSA_GROUP, NSA_HD), lambda b, h: (b, h, 0, 0)),
            pl.BlockSpec((2, None, None, nbc, NSA_HD), lambda b, h: (0, b, h, 0, 0)),
        ],
        out_specs=[
            pl.BlockSpec((None, None, NSA_GROUP, NSA_HD), lambda b, h: (b, h, 0, 0)),
            pl.BlockSpec((None, None, N_SELECT, LANES), lambda b, h: (b, h, 0, 0)),
        ],
        out_shape=[
            jax.ShapeDtypeStruct((batch, kvh, NSA_GROUP, NSA_HD), F32),
            jax.ShapeDtypeStruct((batch, kvh, N_SELECT, LANES), jnp.int32),
        ],
        compiler_params=_cparams(("parallel", "parallel")),
        name="nsa_decode_select",
    )(q, kc)


def _nsa_decode_attend_body(phys_ref, isnew_ref, q_ref, *refs, n_sel, kvh):
    b, h = pl.program_id(0), pl.program_id(1)
    del phys_ref
    blk_refs = refs[:n_sel]
    snew_ref, win_ref, wnew_ref, gate_ref, oc_ref, o_ref = refs[n_sel:]
    q = q_ref[...]
    qb = q.astype(BF16)
    rstride = 2 * kvh

    def attend(kb, vb, valid, k_new, v_new, use_new):
        s = jnp.where(valid, _dot_nt(qb, kb) * ATTN_SCALE, NEG_INF)
        s_n = jnp.where(use_new, jnp.sum(q * k_new, axis=-1, keepdims=True) * ATTN_SCALE, NEG_INF)
        mx = jnp.maximum(jnp.max(s, axis=-1, keepdims=True), s_n)
        e = jnp.where(valid, jnp.exp(s - mx), 0.0)
        e_n = jnp.where(use_new, jnp.exp(s_n - mx), 0.0)
        den = jnp.maximum(jnp.sum(e, axis=-1, keepdims=True) + e_n, 1e-30)
        return (_dot(e.astype(BF16), vb) + e_n * v_new) / den

    lane_blk = lax.broadcasted_iota(jnp.int32, (1, n_sel * BLOCK), 1) // BLOCK
    valid = jnp.zeros((1, n_sel * BLOCK), jnp.int32)
    any_new = jnp.int32(0)
    for k in range(n_sel):
        is_new = isnew_ref[b, h, k]
        valid = jnp.where(lane_blk == k, 1 - is_new, valid)
        any_new = jnp.maximum(any_new, is_new)
    kb = jnp.concatenate([r[pl.ds(2 * h, BLOCK, stride=rstride), :] for r in blk_refs], axis=0)
    vb = jnp.concatenate([r[pl.ds(2 * h + 1, BLOCK, stride=rstride), :] for r in blk_refs], axis=0)
    o_s = attend(kb.astype(BF16), vb.astype(BF16), valid > 0,
                 snew_ref[:, :NSA_HD], snew_ref[:, NSA_HD:], any_new > 0)

    wb = win_ref.shape[0] // rstride
    kw = win_ref[pl.ds(2 * h, wb, stride=rstride), :].astype(BF16)
    vw = win_ref[pl.ds(2 * h + 1, wb, stride=rstride), :].astype(BF16)
    prow = lax.broadcasted_iota(jnp.int32, (1, wb), 1)
    o_w = attend(kw, vw, prow >= 1, wnew_ref[:, :NSA_HD], wnew_ref[:, NSA_HD:], True)

    gates = _sigmoid(gate_ref[...])
    o_ref[...] = gates[0] * oc_ref[...] + gates[1] * o_s + gates[2] * o_w


def _nsa_decode_attend(phys, isnew, q, pool, snew, win, wnew, gate, o_c, *, layer):
    batch, kvh, n_sel = phys.shape
    kv2 = 2 * NSA_HD
    brows = pool.shape[2]
    wrows = win.shape[2]
    def blk_spec(k):
        return pl.BlockSpec((None, None, brows, NSA_HD), lambda b, h, ph, nw: (layer, ph[b, h, k], 0, 0))

    head = pl.BlockSpec((None, None, NSA_GROUP, NSA_HD), lambda b, h, ph, nw: (b, h, 0, 0))
    new_row = pl.BlockSpec((None, None, 1, kv2), lambda b, h, ph, nw: (b, h, 0, 0))
    grid_spec = pltpu.PrefetchScalarGridSpec(
        num_scalar_prefetch=2,
        grid=(batch, kvh),
        in_specs=[head] + [blk_spec(k) for k in range(n_sel)] + [
            new_row,
            pl.BlockSpec((None, None, wrows, NSA_HD), lambda b, h, ph, nw: (layer, b, 0, 0)),
            new_row,
            pl.BlockSpec((None, None, 3, NSA_GROUP, NSA_HD), lambda b, h, ph, nw: (b, h, 0, 0, 0)),
            head,
        ],
        out_specs=head,
    )
    return pl.pallas_call(
        functools.partial(_nsa_decode_attend_body, n_sel=n_sel, kvh=kvh),
        grid_spec=grid_spec,
        out_shape=jax.ShapeDtypeStruct((batch, kvh, NSA_GROUP, NSA_HD), F32),
        compiler_params=_cparams(("parallel", "arbitrary")),
        name="nsa_decode_attend",
    )(phys, isnew, q, *([pool] * n_sel), snew, win, wnew, gate, o_c)


def kernel(x_prompt, x_sample, cache_cmp_kv, cache_sel_kv, state_win_kv, state_hgrn, page_table,
           w_in, w_out, w_up, w_down, cmp_pe, cmp_w, lb_logits, hgrn_norm,
           norm_pre_mix, norm_post_mix, norm_pre_ffn, norm_post_ffn):
    depth = w_in.shape[0]
    bp, tp, d_model = x_prompt.shape
    bs, ts, _ = x_sample.shape
    assert ts == 1, "decode path handles one new token per sequence"
    d_hgrn = hgrn_norm.shape[1]
    heads = d_hgrn // HGRN_DK
    d_nsa = w_out.shape[1] - d_hgrn
    d_ff = w_up.shape[2]
    kvh = d_nsa // NSA_HD // NSA_GROUP
    kv_w = kvh * 2 * NSA_HD
    n_main = 4 * d_hgrn + d_nsa + 3 * kv_w
    assert w_in.shape[2] == n_main + 3 * kvh * NSA_GROUP
    col_q = 4 * d_hgrn
    col_ckv = col_q + d_nsa
    col_skv = col_ckv + kv_w
    col_wkv = col_skv + kv_w
    past_len = page_table.shape[1] * PAGE_SIZE
    wb = state_win_kv.shape[2]
    assert wb == WINDOW and past_len % BLOCK == 0
    nb_p = tp // BLOCK
    nb_past = past_len // BLOCK
    n_blk_s = nb_past + 1

    p_lb = jax.nn.softmax(lb_logits.astype(F32), axis=0)
    lbs = jnp.cumsum(p_lb, axis=0) - p_lb[0:1]
    w_in_t = jnp.swapaxes(w_in, 1, 2)

    def gate_weights(l):
        gate = w_in_t[l, n_main:].reshape(3, kvh, NSA_GROUP, d_model).transpose(1, 0, 2, 3)
        gate = gate.reshape(kvh, 3 * NSA_GROUP, d_model)
        return jnp.pad(gate, ((0, 0), (0, LANES - 3 * NSA_GROUP), (0, 0))).reshape(1, kvh * LANES, d_model)

    def lb_rows(lb):
        lb = lb.reshape(heads, 1, HGRN_DK)
        rows = jnp.concatenate([jnp.log(jnp.maximum(lb, LB_FLOOR)), jnp.log1p(-lb), 1.0 - lb], axis=1)
        return jnp.pad(rows, ((0, 0), (0, 5), (0, 0)))

    kc_len = min(tp, KEY_CHUNK)
    nbp = -(-nb_p // LANES) * LANES
    key_blk = (jnp.arange(tp) // BLOCK).reshape(tp // kc_len, kc_len, 1)
    expand = (jnp.arange(nbp)[None, None, :] == key_blk).astype(BF16)
    pool_cmp = cache_cmp_kv.reshape(depth, cache_cmp_kv.shape[1], PAGE_SIZE * kvh * 2, NSA_HD)
    pool_sel = cache_sel_kv.reshape(depth, cache_sel_kv.shape[1] * (PAGE_SIZE // BLOCK), BLOCK * kvh * 2, NSA_HD)
    win_state = state_win_kv.reshape(depth, bs, wb * kvh * 2, NSA_HD)

    xp = x_prompt.reshape(bp * tp, d_model)
    xs = x_sample.reshape(bs, d_model)
    hp = _rmsnorm_bf16(xp, norm_pre_mix[0])
    hs = _rmsnorm_bf16(xs, norm_pre_mix[0])
    outs = {k: [] for k in ("p_cmp", "p_sel", "p_win", "p_rec", "s_cmp", "s_sel", "s_win", "s_rec")}
    kv5 = lambda a, b, t: a.reshape(b, t, kvh, 2, NSA_HD)

    for l in range(depth):
        w_gate = gate_weights(l)
        w_c = cmp_w[l].reshape(2, BLOCK * NSA_HD, NSA_HD).astype(BF16)
        pe_c = cmp_pe[l].transpose(1, 0, 2)
        lbp = lb_rows(lbs[l])
        gn = hgrn_norm[l].reshape(heads, 1, HGRN_DK)
        nxt = norm_pre_mix[l + 1] if l + 1 < depth else None

        u, us = _matmul(hp, hs, w_in_t, l, n_main, w_t=True)
        ug, ugs = _matmul(hp, hs, w_gate, 0, kvh * LANES, w_t=True)

        o_r, st = _hgrn_prompt(u, lbp, gn, batch=bp, seq=tp, heads=heads, tt=min(tp, 256))
        ckv = u[:, col_ckv:col_skv]
        skv = u[:, col_skv:col_wkv]
        wkv = u[:, col_wkv:n_main]
        xc = ckv.reshape(bp, nb_p, BLOCK, kvh, 2, NSA_HD).transpose(4, 0, 3, 1, 2, 5)
        kc = _compress(xc.reshape(2, bp * kvh * nb_p, BLOCK * NSA_HD),
                       pe_c.reshape(2, 1, BLOCK * NSA_HD), w_c)
        kc = jnp.pad(kc.reshape(2, bp, kvh, nb_p, NSA_HD), ((0, 0), (0, 0), (0, 0), (0, nbp - nb_p), (0, 0)))
        o_b = _nsa_prompt(u, ug, kc[0].astype(BF16), jnp.swapaxes(kc[1], 2, 3).astype(BF16), expand,
                          batch=bp, seq=tp, kvh=kvh, col_q=col_q, col_skv=col_skv, col_wkv=col_wkv)
        mix_p = jnp.concatenate([o_r, o_b], axis=1)
        outs["p_cmp"].append(kv5(ckv, bp, tp))
        outs["p_sel"].append(kv5(skv, bp, tp))
        outs["p_win"].append(kv5(wkv, bp, tp)[:, tp - min(WINDOW, tp):])
        outs["p_rec"].append(st)

        o_r, st = _hgrn_decode(us, lbp, gn, state_hgrn[l], batch=bs, heads=heads)
        q_s = us[:, col_q:col_ckv].reshape(bs, kvh, NSA_GROUP, NSA_HD)
        ckv = us[:, col_ckv:col_skv]
        skv = us[:, col_skv:col_wkv]
        wkv = us[:, col_wkv:n_main]
        kc_s = _compress_paged(pool_cmp, page_table, pe_c, w_c, layer=l, kvh=kvh)
        o_c, idx = _nsa_decode_select(q_s, kc_s, past_len=past_len, n_blk=n_blk_s)
        idx = idx[..., 0]
        ppos = jnp.minimum(idx, nb_past - 1) * BLOCK
        page = jnp.take_along_axis(page_table, (ppos // PAGE_SIZE).reshape(bs, -1), axis=1)
        phys = page.reshape(idx.shape) * (PAGE_SIZE // BLOCK) + (ppos % PAGE_SIZE) // BLOCK
        isnew = (idx >= nb_past).astype(jnp.int32)
        gate_s = ugs.reshape(bs, kvh, LANES)[:, :, :3 * NSA_GROUP]
        gate_s = jnp.broadcast_to(gate_s.reshape(bs, kvh, 3, NSA_GROUP, 1), (bs, kvh, 3, NSA_GROUP, NSA_HD))
        o_b = _nsa_decode_attend(phys.astype(jnp.int32), isnew, q_s, pool_sel,
                                 skv.reshape(bs, kvh, 1, 2 * NSA_HD), win_state,
                                 wkv.reshape(bs, kvh, 1, 2 * NSA_HD), gate_s, o_c, layer=l)
        mix_s = jnp.concatenate([o_r, o_b.reshape(bs, d_nsa)], axis=1).astype(BF16)
        outs["s_cmp"].append(kv5(ckv, bs, ts))
        outs["s_sel"].append(kv5(skv, bs, ts))
        outs["s_win"].append(jnp.concatenate([state_win_kv[l][:, ts:], kv5(wkv, bs, ts)], axis=1))
        outs["s_rec"].append(st)

        y, ys = _matmul(mix_p, mix_s, w_out, l, d_model)
        xp, fp = _norm_res(y, norm_post_mix[l], xp, norm_pre_ffn[l])
        xs, fs = _norm_res(ys, norm_post_mix[l], xs, norm_pre_ffn[l])
        fp, fs = _matmul(fp, fs, w_up, l, d_ff, act=True, out_dtype=BF16)
        y, ys = _matmul(fp, fs, w_down, l, d_model)
        if nxt is None:
            xp = _norm_res(y, norm_post_ffn[l], xp)
            xs = _norm_res(ys, norm_post_ffn[l], xs)
        else:
            xp, hp = _norm_res(y, norm_post_ffn[l], xp, nxt)
            xs, hs = _norm_res(ys, norm_post_ffn[l], xs, nxt)

    stack = lambda k: jnp.stack(outs[k])
    return (xp.reshape(bp, tp, d_model), xs.reshape(bs, ts, d_model),
            stack("p_cmp"), stack("p_sel"), stack("p_win"), stack("p_rec"),
            stack("s_cmp"), stack("s_sel"), stack("s_win"), stack("s_rec"))
```

```python
import functools
import math

import jax
import jax.numpy as jnp
from jax import lax
from jax.experimental import pallas as pl
from jax.experimental.pallas import tpu as pltpu

F32 = jnp.float32
BF16 = jnp.bfloat16

HGRN_DK = 128
NSA_HD = 128
NSA_GROUP = 4
BLOCK = 64
N_SELECT = 16
WINDOW = 512
PAGE_SIZE = 128
EPS = 1e-6
NEG_INF = -1e30
LB_FLOOR = 1e-30
ATTN_SCALE = NSA_HD ** -0.5
FORCE_SCORE = NSA_GROUP + 1.0

LANES = 128
VMEM_LIMIT = 56 * 1024 * 1024

HGRN_CHUNK = 64
HGRN_SUB = 8
HGRN_HEADS_PER_STEP = 4
PAGES_PER_STEP = 16
KEY_CHUNK = 512


def _cparams(sem):
    return pltpu.CompilerParams(dimension_semantics=sem, vmem_limit_bytes=VMEM_LIMIT)


def _dot(a, b):
    return jnp.dot(a, b, preferred_element_type=F32)


def _dot_nt(a, b):
    return lax.dot_general(a, b, (((1,), (1,)), ((), ())), preferred_element_type=F32)


def _dot_tn(a, b):
    return lax.dot_general(a, b, (((0,), (0,)), ((), ())), preferred_element_type=F32)


def _sigmoid(x):
    return 1.0 / (1.0 + jnp.exp(-x))


def _silu(x):
    return x * _sigmoid(x)


def _rms(x, g):
    ms = jnp.mean(x * x, axis=-1, keepdims=True)
    return x * lax.rsqrt(ms + EPS) * g


def _masked_softmax(s, mask):
    s = jnp.where(mask, s, NEG_INF)
    e = jnp.where(mask, jnp.exp(s - jnp.max(s, axis=-1, keepdims=True)), 0.0)
    return e / jnp.maximum(jnp.sum(e, axis=-1, keepdims=True), 1e-30)


def _rmsnorm_body(x_ref, g_ref, o_ref):
    o_ref[...] = _rms(x_ref[...], g_ref[...]).astype(o_ref.dtype)


def _rmsnorm_bf16(x, g):
    m, d = x.shape
    tm = min(m, 256)
    return pl.pallas_call(
        _rmsnorm_body,
        grid=(m // tm,),
        in_specs=[pl.BlockSpec((tm, d), lambda i: (i, 0)), pl.BlockSpec((1, d), lambda i: (0, 0))],
        out_specs=pl.BlockSpec((tm, d), lambda i: (i, 0)),
        out_shape=jax.ShapeDtypeStruct((m, d), BF16),
        compiler_params=_cparams(("parallel",)),
        name="rmsnorm",
    )(x, g.reshape(1, d))


def _norm_res_body(y_ref, g1_ref, r_ref, *rest):
    out = r_ref[...] + _rms(y_ref[...], g1_ref[...])
    if len(rest) == 1:
        rest[0][...] = out
    else:
        g2_ref, o_ref, n_ref = rest
        o_ref[...] = out
        n_ref[...] = _rms(out, g2_ref[...]).astype(n_ref.dtype)


def _norm_res(y, g1, res, g2=None):
    m, d = y.shape
    tm = min(m, 256)
    row = pl.BlockSpec((tm, d), lambda i: (i, 0))
    vec = pl.BlockSpec((1, d), lambda i: (0, 0))
    if g2 is None:
        return pl.pallas_call(
            _norm_res_body, grid=(m // tm,), in_specs=[row, vec, row], out_specs=row,
            out_shape=jax.ShapeDtypeStruct((m, d), F32),
            compiler_params=_cparams(("parallel",)), name="norm_res",
        )(y, g1.reshape(1, d), res)
    return pl.pallas_call(
        _norm_res_body, grid=(m // tm,), in_specs=[row, vec, row, vec], out_specs=[row, row],
        out_shape=[jax.ShapeDtypeStruct((m, d), F32), jax.ShapeDtypeStruct((m, d), BF16)],
        compiler_params=_cparams(("parallel",)), name="norm_res_norm",
    )(y, g1.reshape(1, d), res, g2.reshape(1, d))


def _matmul_body(x_ref, xs_ref, w_ref, o_ref, os_ref, *wb_ref, act, nk, w_t):
    i, kk = pl.program_id(1), pl.program_id(2)
    mm = _dot_nt if w_t else _dot

    def finish(part, ref, first):
        if nk == 1:
            if act:
                part = jnp.square(jnp.maximum(part, 0.0))
            ref[...] = part.astype(ref.dtype)
        else:
            @pl.when(first)
            def _():
                ref[...] = part

            @pl.when(jnp.logical_not(first))
            def _():
                ref[...] += part

    if nk == 1:
        @pl.when(i == 0)
        def _():
            wb_ref[0][...] = w_ref[...].astype(BF16)
        wb = wb_ref[0][...]
    else:
        wb = w_ref[...].astype(BF16)

    finish(mm(x_ref[...], wb), o_ref, kk == 0)

    @pl.when(i == 0)
    def _():
        finish(mm(xs_ref[...], wb), os_ref, kk == 0)


def _matmul(x, xs, w, layer, n_out, *, act=False, out_dtype=F32, w_t=False):
    m, k = x.shape
    ms = xs.shape[0]
    tm = min(m, 1024)
    tk, tn = (k, 512) if k <= 4096 else (2048, 1024)
    tn = math.gcd(tn, n_out)
    nk = k // tk
    assert m % tm == 0 and n_out % tn == 0 and k % tk == 0
    assert nk == 1 or (out_dtype == F32 and not act)
    if w_t:
        w_spec = pl.BlockSpec((None, tn, tk), lambda j, i, kk: (layer, j, kk))
        wb_shape = (tn, tk)
    else:
        w_spec = pl.BlockSpec((None, tk, tn), lambda j, i, kk: (layer, kk, j))
        wb_shape = (tk, tn)
    return pl.pallas_call(
        functools.partial(_matmul_body, act=act, nk=nk, w_t=w_t),
        grid=(n_out // tn, m // tm, nk),
        in_specs=[
            pl.BlockSpec((tm, tk), lambda j, i, kk: (i, kk)),
            pl.BlockSpec((ms, tk), lambda j, i, kk: (0, kk)),
            w_spec,
        ],
        out_specs=[
            pl.BlockSpec((tm, tn), lambda j, i, kk: (i, j)),
            pl.BlockSpec((ms, tn), lambda j, i, kk: (0, j)),
        ],
        out_shape=[
            jax.ShapeDtypeStruct((m, n_out), out_dtype),
            jax.ShapeDtypeStruct((ms, n_out), out_dtype),
        ],
        scratch_shapes=[pltpu.VMEM(wb_shape, BF16)] if nk == 1 else [],
        compiler_params=_cparams(("arbitrary", "arbitrary", "arbitrary")),
        name="matmul",
    )(x, xs, w)


def _hgrn_gates(z, lbp):
    a = lbp[0:1, :]
    log_sig = jnp.minimum(z, 0.0) - jnp.log1p(jnp.exp(-jnp.abs(z)))
    b = lbp[1:2, :] + log_sig
    logf = jnp.maximum(a, b) + jnp.log1p(jnp.exp(-jnp.abs(a - b)))
    k = lbp[2:3, :] * _sigmoid(-z)
    return logf, k


def _hgrn_out(o, gn, hg):
    return _rms(o, gn) * _silu(hg)


def _hgrn_chunk(q, z, v, hg, lbp, gn, st, consts):
    c, sub = HGRN_CHUNK, HGRN_SUB
    ns = c // sub
    tri, ones, t_in_sub = consts
    q = _silu(q)
    logf, k = _hgrn_gates(z, lbp)

    hi = logf.astype(BF16)
    r1 = logf - hi.astype(F32)
    mid = r1.astype(BF16)
    lo = (r1 - mid.astype(F32)).astype(BF16)
    cb = _dot(tri, hi) + _dot(tri, mid) + _dot(tri, lo)

    q3 = q.reshape(ns, sub, HGRN_DK)
    k3 = k.reshape(ns, sub, HGRN_DK)
    v3 = v.reshape(ns, sub, HGRN_DK)
    cb3 = cb.reshape(ns, sub, HGRN_DK)
    e_end = cb3[:, sub - 1:sub, :]
    b_beg = jnp.concatenate([jnp.zeros((1, 1, HGRN_DK), F32), e_end[:ns - 1]], axis=0)
    qt = q3 * jnp.exp(cb3 - b_beg)
    kt = k3 * jnp.exp(e_end - cb3)

    q_cols, k_cols = [], []
    zeros_sub = jnp.zeros((sub, HGRN_DK), F32)
    for j in range(ns - 1):
        qc, kc = [], []
        for i in range(ns):
            qc.append(qt[i] * jnp.exp(b_beg[i] - e_end[j]) if i > j else zeros_sub)
            kc.append(kt[i] if i == j else zeros_sub)
        q_cols.append(jnp.concatenate(qc, axis=0))
        k_cols.append(jnp.concatenate(kc, axis=0))
    q_hat = jnp.concatenate(q_cols, axis=1).astype(BF16)
    k_hat = jnp.concatenate(k_cols, axis=1).astype(BF16)
    att_off = _dot_nt(q_hat, k_hat)
    o = _dot(att_off.astype(BF16), v.astype(BF16))

    parts = []
    for s in range(sub):
        dec = jnp.exp(cb3 - cb3[:, s:s + 1, :])
        p = jnp.where(t_in_sub >= s, q3 * k3[:, s:s + 1, :] * dec, 0.0)
        parts.append(p.reshape(c, HGRN_DK).astype(BF16))
    rsum = _dot(jnp.concatenate(parts, axis=0), ones)
    o_diag = jnp.zeros((ns, sub, HGRN_DK), F32)
    for s in range(sub):
        o_diag = o_diag + rsum[s * c:(s + 1) * c, :].reshape(ns, sub, HGRN_DK) * v3[:, s:s + 1, :]
    o = o + o_diag.reshape(c, HGRN_DK)

    o = o + _dot_nt((q * jnp.exp(cb)).astype(BF16), st.astype(BF16))
    last = cb[c - 1:c, :]
    kd = (k * jnp.exp(last - cb)).astype(BF16)
    st_new = st * jnp.exp(last) + _dot_tn(v.astype(BF16), kd)
    return _hgrn_out(o, gn, hg), st_new


def _hgrn_prompt_body(q_ref, f_ref, i_ref, g_ref, lb_ref, gn_ref, o_ref, st_ref, s_ref, *, n_chunks):
    c, sub = HGRN_CHUNK, HGRN_SUB
    hp = s_ref.shape[0]
    tstep = pl.program_id(2)

    @pl.when(tstep == 0)
    def _():
        s_ref[...] = jnp.zeros_like(s_ref)

    row = lax.broadcasted_iota(jnp.int32, (c, c), 0)
    col = lax.broadcasted_iota(jnp.int32, (c, c), 1)
    consts = (
        jnp.where(row >= col, 1.0, 0.0).astype(BF16),
        jnp.ones((LANES, LANES), BF16),
        lax.broadcasted_iota(jnp.int32, (c // sub, sub, HGRN_DK), 1),
    )

    def chunk(ci, carry):
        rows = pl.ds(pl.multiple_of(ci * c, c), c)
        for hh in range(hp):
            cols = slice(hh * HGRN_DK, (hh + 1) * HGRN_DK)
            out, st_new = _hgrn_chunk(q_ref[rows, cols], f_ref[rows, cols], i_ref[rows, cols],
                                      g_ref[rows, cols], lb_ref[hh], gn_ref[hh], s_ref[hh], consts)
            s_ref[hh] = st_new
            o_ref[rows, cols] = out.astype(o_ref.dtype)
        return carry

    lax.fori_loop(0, n_chunks, chunk, 0)

    @pl.when(tstep == pl.num_programs(2) - 1)
    def _():
        for hh in range(hp):
            st_ref[hh] = s_ref[hh].T


def _hgrn_prompt(u, lbp, gn, *, batch, seq, heads, tt):
    nt = seq // tt
    hp = math.gcd(heads, HGRN_HEADS_PER_STEP)
    w = hp * HGRN_DK
    ng = heads // hp
    blk = lambda part: pl.BlockSpec((tt, w), lambda b, h, t: (b * nt + t, part * ng + h))
    return pl.pallas_call(
        functools.partial(_hgrn_prompt_body, n_chunks=tt // HGRN_CHUNK),
        grid=(batch, ng, nt),
        in_specs=[
            blk(0), blk(1), blk(2), blk(3),
            pl.BlockSpec((hp, 8, HGRN_DK), lambda b, h, t: (h, 0, 0)),
            pl.BlockSpec((hp, 1, HGRN_DK), lambda b, h, t: (h, 0, 0)),
        ],
        out_specs=[
            pl.BlockSpec((tt, w), lambda b, h, t: (b * nt + t, h)),
            pl.BlockSpec((None, hp, HGRN_DK, HGRN_DK), lambda b, h, t: (b, h, 0, 0)),
        ],
        out_shape=[
            jax.ShapeDtypeStruct((batch * seq, heads * HGRN_DK), BF16),
            jax.ShapeDtypeStruct((batch, heads, HGRN_DK, HGRN_DK), F32),
        ],
        scratch_shapes=[pltpu.VMEM((hp, HGRN_DK, HGRN_DK), F32)],
        compiler_params=_cparams(("parallel", "parallel", "arbitrary")),
        name="hgrn_prompt",
    )(u, u, u, u, lbp, gn)


def _hgrn_decode_body(q_ref, f_ref, i_ref, g_ref, lb_ref, gn_ref, s0_ref, o_ref, st_ref):
    r = pl.ds(pl.program_id(1), 1)
    logf, k = _hgrn_gates(f_ref[r, :], lb_ref[...])
    q = _silu(q_ref[r, :])
    v = i_ref[r, :]

    def col(x):
        return jnp.broadcast_to(x, (HGRN_DK, HGRN_DK)).T

    s_new = col(jnp.exp(logf)) * s0_ref[...] + col(k) * v
    st_ref[...] = s_new
    o = jnp.sum(col(q) * s_new, axis=0, keepdims=True)
    o_ref[r, :] = _hgrn_out(o, gn_ref[...], g_ref[r, :])


def _hgrn_decode(u, lbp, gn, s0, *, batch, heads):
    blk = lambda off: pl.BlockSpec((batch, HGRN_DK), lambda h, b: (0, off + h))
    return pl.pallas_call(
        _hgrn_decode_body,
        grid=(heads, batch),
        in_specs=[
            blk(0), blk(heads), blk(2 * heads), blk(3 * heads),
            pl.BlockSpec((None, 8, HGRN_DK), lambda h, b: (h, 0, 0)),
            pl.BlockSpec((None, 1, HGRN_DK), lambda h, b: (h, 0, 0)),
            pl.BlockSpec((None, None, HGRN_DK, HGRN_DK), lambda h, b: (b, h, 0, 0)),
        ],
        out_specs=[
            pl.BlockSpec((batch, HGRN_DK), lambda h, b: (0, h)),
            pl.BlockSpec((None, None, HGRN_DK, HGRN_DK), lambda h, b: (b, h, 0, 0)),
        ],
        out_shape=[
            jax.ShapeDtypeStruct((batch, heads * HGRN_DK), F32),
            jax.ShapeDtypeStruct(s0.shape, F32),
        ],
        compiler_params=_cparams(("parallel", "arbitrary")),
        name="hgrn_decode",
    )(u, u, u, u, lbp, gn, s0)


def _kv_rows_body(*refs, hc):
    n = len(refs) // 2
    for x_ref, o_ref in zip(refs[:n], refs[n:]):
        tm = x_ref.shape[0]
        for j in range(hc):
            o_ref[pl.ds(j, tm, stride=hc), :] = x_ref[:, j * NSA_HD:(j + 1) * NSA_HD]


def _kv_rows(u, cols, kv_w):
    m = u.shape[0]
    tm = min(m, 256)
    hc = kv_w // NSA_HD
    assert all(c % kv_w == 0 for c in cols)
    return pl.pallas_call(
        functools.partial(_kv_rows_body, hc=hc),
        grid=(m // tm,),
        in_specs=[pl.BlockSpec((tm, kv_w), lambda i, c=c: (i, c // kv_w)) for c in cols],
        out_specs=[pl.BlockSpec((tm * hc, NSA_HD), lambda i: (i, 0)) for _ in cols],
        out_shape=[jax.ShapeDtypeStruct((m * hc, NSA_HD), F32) for _ in cols],
        compiler_params=_cparams(("parallel",)),
        name="kv_rows",
    )(*([u] * len(cols)))


def _compress_paged_body(pt_ref, *refs, kvh, npg):
    del pt_ref
    page_refs = refs[:npg]
    pe_ref, w_ref, o_ref, lhs_ref = refs[npg:]
    hc = 2 * kvh
    bpp = PAGE_SIZE // BLOCK
    nblk = npg * bpp
    for r in range(BLOCK):
        rows = [page_refs[p][(half * BLOCK + r) * hc:(half * BLOCK + r + 1) * hc, :]
                for p in range(npg) for half in range(bpp)]
        x = jnp.concatenate(rows, axis=0).reshape(nblk, hc, NSA_HD) + pe_ref[r]
        lhs_ref[:, r * NSA_HD:(r + 1) * NSA_HD] = x.reshape(nblk * hc, NSA_HD).astype(BF16)
    y = _dot(lhs_ref[...], w_ref[...])
    is_v = lax.broadcasted_iota(jnp.int32, (nblk * hc, NSA_HD), 0) % 2 == 1
    out = jnp.where(is_v, y[:, NSA_HD:], y[:, :NSA_HD])
    o_ref[...] = out.reshape(nblk, hc, NSA_HD)


def _compress_paged(pool, page_table, pe, w, *, layer, kvh):
    batch, n_pages = page_table.shape
    npg = math.gcd(n_pages, PAGES_PER_STEP)
    rpp = pool.shape[2]
    hc = 2 * kvh
    nblk = npg * PAGE_SIZE // BLOCK

    def page_spec(p):
        return pl.BlockSpec((None, None, rpp, NSA_HD),
                            lambda b, g, pt: (layer, pt[b, g * npg + p], 0, 0))

    grid_spec = pltpu.PrefetchScalarGridSpec(
        num_scalar_prefetch=1,
        grid=(batch, n_pages // npg),
        in_specs=[page_spec(p) for p in range(npg)] + [
            pl.BlockSpec((BLOCK, hc, NSA_HD), lambda b, g, pt: (0, 0, 0)),
            pl.BlockSpec((BLOCK * NSA_HD, 2 * NSA_HD), lambda b, g, pt: (0, 0)),
        ],
        out_specs=pl.BlockSpec((None, nblk, hc, NSA_HD), lambda b, g, pt: (b, g, 0, 0)),
        scratch_shapes=[pltpu.VMEM((nblk * hc, BLOCK * NSA_HD), BF16)],
    )
    return pl.pallas_call(
        functools.partial(_compress_paged_body, kvh=kvh, npg=npg),
        grid_spec=grid_spec,
        out_shape=jax.ShapeDtypeStruct((batch, n_pages * PAGE_SIZE // BLOCK, hc, NSA_HD), F32),
        compiler_params=_cparams(("parallel", "arbitrary")),
        name="compress_paged",
    )(page_table, *([pool] * npg), pe, w)


def _nsa_prompt_body(q_ref, kc_ref, vct_ref, skv_ref, wkv_ref, gate_ref, e_ref, o_ref,
                     ks_ref, vst_ref, kw_ref, vwt_ref, m_ref, l_ref, acc_ref, *, tq, seq, n_blk, span, kc_len):
    i = pl.program_id(2)

    @pl.when(i == 0)
    def _():
        ks_ref[...] = skv_ref[:, :NSA_HD].astype(BF16)
        kw_ref[...] = wkv_ref[:, :NSA_HD].astype(BF16)
        for c in range(seq // kc_len):
            vst_ref[c] = skv_ref[c * kc_len:(c + 1) * kc_len, NSA_HD:].T.astype(BF16)
        for c in range(seq // tq):
            vwt_ref[c] = wkv_ref[c * tq:(c + 1) * tq, NSA_HD:].T.astype(BF16)

    nbp = kc_ref.shape[0]
    nb8 = -(-n_blk // 8) * 8
    n_sel = min(N_SELECT, n_blk)
    q0 = i * tq
    qpos = q0 + lax.broadcasted_iota(jnp.int32, (1, tq), 1)
    cur = qpos // BLOCK
    n_sub = lax.broadcasted_iota(jnp.int32, (nbp, 1), 0)
    cmask = (n_sub * BLOCK + (BLOCK - 1) <= qpos) & (n_sub < seq // BLOCK)

    kc = kc_ref[...]
    vct = vct_ref[...]
    q_all = jnp.concatenate(
        [(q_ref[:, g * NSA_HD:(g + 1) * NSA_HD] * ATTN_SCALE).astype(BF16) for g in range(NSA_GROUP)], axis=0)
    per_head = lambda a: jnp.concatenate([a] * NSA_GROUP, axis=1)
    head = lambda a, g: a[:, g * tq:(g + 1) * tq]

    cmask4 = per_head(cmask)
    s = jnp.where(cmask4, _dot_nt(kc, q_all), NEG_INF)
    e = jnp.where(cmask4, jnp.exp(s - jnp.max(s, axis=0, keepdims=True)), 0.0)
    p = e / jnp.maximum(jnp.sum(e, axis=0, keepdims=True), 1e-30)
    o_c = _dot(vct, p.astype(BF16))
    imp = head(p, 0)
    for g in range(1, NSA_GROUP):
        imp = imp + head(p, g)

    forced = (n_sub == 0) | (n_sub == cur) | (n_sub == cur - 1)
    score = jnp.where(n_sub <= cur, jnp.where(forced, FORCE_SCORE, imp), -1.0)[:nb8, :]
    j_idx = lax.broadcasted_iota(jnp.int32, (nb8, tq), 0)
    score = jnp.where(j_idx < n_blk, score, -2.0)
    cnt = jnp.zeros((nb8, tq), F32)
    for j in range(n_blk):
        rj = score[j:j + 1, :]
        beats = jnp.where(rj > score, 1.0, jnp.where((rj == score) & (j_idx > j), 1.0, 0.0))
        cnt = cnt + beats
    sel = jnp.where((cnt < n_sel) & (j_idx <= cur) & (j_idx < n_blk), 1.0, 0.0)
    if nbp > nb8:
        sel = jnp.concatenate([sel, jnp.zeros((nbp - nb8, tq), F32)], axis=0)
    sel = sel.astype(BF16)

    m_ref[...] = jnp.full_like(m_ref, NEG_INF)
    l_ref[...] = jnp.zeros_like(l_ref)
    acc_ref[...] = jnp.zeros_like(acc_ref)

    def key_chunk(ci, carry):
        k0 = pl.multiple_of(ci * kc_len, kc_len)
        kk = ks_ref[pl.ds(k0, kc_len), :]
        vt = vst_ref[ci]
        kpos = k0 + lax.broadcasted_iota(jnp.int32, (kc_len, 1), 0)
        bias = jnp.where((_dot(e_ref[ci], sel) > 0.5) & (kpos <= qpos), 0.0, NEG_INF)
        s = _dot_nt(kk, q_all) + per_head(bias)
        m_old = m_ref[...]
        m_new = jnp.maximum(m_old, jnp.max(s, axis=0, keepdims=True))
        alpha = jnp.exp(m_old - m_new)
        p = jnp.exp(s - m_new)
        l_ref[...] = alpha * l_ref[...] + jnp.sum(p, axis=0, keepdims=True)
        acc_ref[...] = alpha * acc_ref[...] + _dot(vt, p.astype(BF16))
        m_ref[...] = m_new
        return carry

    lax.fori_loop(0, (q0 + tq + kc_len - 1) // kc_len, key_chunk, 0)

    nwb = span // tq
    blk0 = jnp.maximum(i + 1 - nwb, 0)
    start = pl.multiple_of(blk0 * tq, tq)
    wpos = start + lax.broadcasted_iota(jnp.int32, (span, 1), 0)
    dist = qpos - wpos
    wbias = jnp.where((dist >= 0) & (dist < WINDOW), 0.0, NEG_INF)
    kw = kw_ref[pl.ds(start, span), :]
    vwt = jnp.concatenate([vwt_ref[blk0 + j] for j in range(nwb)], axis=1)

    o_s = acc_ref[...] / l_ref[...]
    s = _dot_nt(kw, q_all) + per_head(wbias)
    e = jnp.exp(s - jnp.max(s, axis=0, keepdims=True))
    o_w = _dot(vwt, e.astype(BF16)) / jnp.sum(e, axis=0, keepdims=True)

    gates = _sigmoid(gate_ref[...]).T
    for g in range(NSA_GROUP):
        gc = gates[g:g + 1, :]
        gs = gates[NSA_GROUP + g:NSA_GROUP + g + 1, :]
        gw = gates[2 * NSA_GROUP + g:2 * NSA_GROUP + g + 1, :]
        o = gc * head(o_c, g) + gs * head(o_s, g) + gw * head(o_w, g)
        o_ref[:, g * NSA_HD:(g + 1) * NSA_HD] = o.T.astype(o_ref.dtype)


def _nsa_prompt(u, ug, kc, vct, expand, *, batch, seq, kvh, col_q, col_skv, col_wkv):
    tq = min(seq, LANES)
    assert tq == LANES or seq < LANES
    nq = seq // tq
    span = min(seq, WINDOW + tq)
    qw = NSA_GROUP * NSA_HD
    nbp = kc.shape[2]
    n_kc, kc_len, _ = expand.shape
    return pl.pallas_call(
        functools.partial(_nsa_prompt_body, tq=tq, seq=seq, n_blk=seq // BLOCK, span=span, kc_len=kc_len),
        grid=(batch, kvh, nq),
        in_specs=[
            pl.BlockSpec((tq, qw), lambda b, h, i: (b * nq + i, col_q // qw + h)),
            pl.BlockSpec((None, None, nbp, NSA_HD), lambda b, h, i: (b, h, 0, 0)),
            pl.BlockSpec((None, None, NSA_HD, nbp), lambda b, h, i: (b, h, 0, 0)),
            pl.BlockSpec((seq, 2 * NSA_HD), lambda b, h, i: (b, col_skv // (2 * NSA_HD) + h)),
            pl.BlockSpec((seq, 2 * NSA_HD), lambda b, h, i: (b, col_wkv // (2 * NSA_HD) + h)),
            pl.BlockSpec((tq, LANES), lambda b, h, i: (b * nq + i, h)),
            pl.BlockSpec((n_kc, kc_len, nbp), lambda b, h, i: (0, 0, 0)),
        ],
        out_specs=pl.BlockSpec((tq, qw), lambda b, h, i: (b * nq + i, h)),
        out_shape=jax.ShapeDtypeStruct((batch * seq, kvh * qw), BF16),
        scratch_shapes=[
            pltpu.VMEM((seq, NSA_HD), BF16),
            pltpu.VMEM((n_kc, NSA_HD, kc_len), BF16),
            pltpu.VMEM((seq, NSA_HD), BF16),
            pltpu.VMEM((nq, NSA_HD, tq), BF16),
            pltpu.VMEM((1, NSA_GROUP * tq), F32),
            pltpu.VMEM((1, NSA_GROUP * tq), F32),
            pltpu.VMEM((NSA_HD, NSA_GROUP * tq), F32),
        ],
        compiler_params=_cparams(("parallel", "parallel", "arbitrary")),
        name="nsa_prompt",
    )(u, kc, vct, u, u, ug, expand)


def _nsa_decode_select_body(q_ref, kc_ref, oc_ref, idx_ref, *, past_len, n_blk):
    nbc = kc_ref.shape[1]
    npad = -(-n_blk // LANES) * LANES
    n_sel = min(N_SELECT, n_blk)
    qpos = past_len
    cur = qpos // BLOCK
    q = q_ref[...].astype(BF16)
    n_lane = lax.broadcasted_iota(jnp.int32, (1, nbc), 1)
    cmask = n_lane * BLOCK + (BLOCK - 1) <= qpos
    p = _masked_softmax(_dot_nt(q, kc_ref[0].astype(BF16)) * ATTN_SCALE, cmask)
    oc_ref[...] = _dot(p.astype(BF16), kc_ref[1].astype(BF16))
    imp = jnp.sum(p, axis=0, keepdims=True)
    imp = jnp.concatenate([imp, jnp.zeros((1, npad - nbc), F32)], axis=1)

    j_lane = lax.broadcasted_iota(jnp.int32, (1, npad), 1)
    forced = (j_lane == 0) | (j_lane == cur) | (j_lane == cur - 1)
    score = jnp.where(j_lane <= cur, jnp.where(forced, FORCE_SCORE, imp), -1.0)
    score = jnp.where(j_lane < n_blk, score, -2.0)
    s_row = jnp.broadcast_to(score, (npad, npad))
    s_col = s_row.T
    jp = lax.broadcasted_iota(jnp.int32, (npad, npad), 0)
    jj = lax.broadcasted_iota(jnp.int32, (npad, npad), 1)
    beats = jnp.where(s_col > s_row, 1.0, jnp.where((s_col == s_row) & (jp < jj), 1.0, 0.0))
    rank = jnp.sum(beats, axis=0, keepdims=True)
    k_sub = lax.broadcasted_iota(jnp.int32, (N_SELECT, npad), 0).astype(F32)
    j_f = lax.broadcasted_iota(jnp.int32, (N_SELECT, npad), 1).astype(F32)
    hit = jnp.where((rank == k_sub) & (j_f < n_blk), j_f, 0.0)
    idx = jnp.sum(hit, axis=1, keepdims=True)
    idx = jnp.where(k_sub[:, :1] < n_sel, idx, 0.0)
    idx_ref[...] = jnp.broadcast_to(idx, (N_SELECT, LANES)).astype(jnp.int32)


def _nsa_decode_select(q, kc, *, past_len, n_blk):
    batch, kvh = q.shape[:2]
    nbc = kc.shape[3]
    return pl.pallas_call(
        functools.partial(_nsa_decode_select_body, past_len=past_len, n_blk=n_blk),
        grid=(batch, kvh),
        in_specs=[
            pl.BlockSpec((None, None, NSA_GROUP, NSA_HD), lambda b, h: (b, h, 0, 0)),
            pl.BlockSpec((2, None, None, nbc, NSA_HD), lambda b, h: (0, b, h, 0, 0)),
        ],
        out_specs=[
            pl.BlockSpec((None, None, NSA_GROUP, NSA_HD), lambda b, h: (b, h, 0, 0)),
            pl.BlockSpec((None, None, N_SELECT, LANES), lambda b, h: (b, h, 0, 0)),
        ],
        out_shape=[
            jax.ShapeDtypeStruct((batch, kvh, NSA_GROUP, NSA_HD), F32),
            jax.ShapeDtypeStruct((batch, kvh, N_SELECT, LANES), jnp.int32),
        ],
        compiler_params=_cparams(("parallel", "parallel")),
        name="nsa_decode_select",
    )(q, kc)


def _nsa_decode_attend_body(phys_ref, isnew_ref, q_ref, *refs, n_sel, kvh):
    b, h = pl.program_id(0), pl.program_id(1)
    del phys_ref
    blk_refs = refs[:n_sel]
    snew_ref, win_ref, wnew_ref, gate_ref, oc_ref, o_ref = refs[n_sel:]
    q = q_ref[...]
    qb = q.astype(BF16)
    rstride = 2 * kvh

    def attend(kb, vb, valid, k_new, v_new, use_new):
        s = jnp.where(valid, _dot_nt(qb, kb) * ATTN_SCALE, NEG_INF)
        s_n = jnp.where(use_new, jnp.sum(q * k_new, axis=-1, keepdims=True) * ATTN_SCALE, NEG_INF)
        mx = jnp.maximum(jnp.max(s, axis=-1, keepdims=True), s_n)
        e = jnp.where(valid, jnp.exp(s - mx), 0.0)
        e_n = jnp.where(use_new, jnp.exp(s_n - mx), 0.0)
        den = jnp.maximum(jnp.sum(e, axis=-1, keepdims=True) + e_n, 1e-30)
        return (_dot(e.astype(BF16), vb) + e_n * v_new) / den

    lane_blk = lax.broadcasted_iota(jnp.int32, (1, n_sel * BLOCK), 1) // BLOCK
    valid = jnp.zeros((1, n_sel * BLOCK), jnp.int32)
    any_new = jnp.int32(0)
    for k in range(n_sel):
        is_new = isnew_ref[b, h, k]
        valid = jnp.where(lane_blk == k, 1 - is_new, valid)
        any_new = jnp.maximum(any_new, is_new)
    kb = jnp.concatenate([r[pl.ds(2 * h, BLOCK, stride=rstride), :] for r in blk_refs], axis=0)
    vb = jnp.concatenate([r[pl.ds(2 * h + 1, BLOCK, stride=rstride), :] for r in blk_refs], axis=0)
    o_s = attend(kb.astype(BF16), vb.astype(BF16), valid > 0,
                 snew_ref[:, :NSA_HD], snew_ref[:, NSA_HD:], any_new > 0)

    wb = win_ref.shape[0] // rstride
    kw = win_ref[pl.ds(2 * h, wb, stride=rstride), :].astype(BF16)
    vw = win_ref[pl.ds(2 * h + 1, wb, stride=rstride), :].astype(BF16)
    prow = lax.broadcasted_iota(jnp.int32, (1, wb), 1)
    o_w = attend(kw, vw, prow >= 1, wnew_ref[:, :NSA_HD], wnew_ref[:, NSA_HD:], True)

    gates = _sigmoid(gate_ref[...])
    o_ref[...] = gates[0] * oc_ref[...] + gates[1] * o_s + gates[2] * o_w


def _nsa_decode_attend(phys, isnew, q, pool, snew, win, wnew, gate, o_c, *, layer):
    batch, kvh, n_sel = phys.shape
    kv2 = 2 * NSA_HD
    brows = pool.shape[2]
    wrows = win.shape[2]
    def blk_spec(k):
        return pl.BlockSpec((None, None, brows, NSA_HD), lambda b, h, ph, nw: (layer, ph[b, h, k], 0, 0))

    head = pl.BlockSpec((None, None, NSA_GROUP, NSA_HD), lambda b, h, ph, nw: (b, h, 0, 0))
    new_row = pl.BlockSpec((None, None, 1, kv2), lambda b, h, ph, nw: (b, h, 0, 0))
    grid_spec = pltpu.PrefetchScalarGridSpec(
        num_scalar_prefetch=2,
        grid=(batch, kvh),
        in_specs=[head] + [blk_spec(k) for k in range(n_sel)] + [
            new_row,
            pl.BlockSpec((None, None, wrows, NSA_HD), lambda b, h, ph, nw: (layer, b, 0, 0)),
            new_row,
            pl.BlockSpec((None, None, 3, NSA_GROUP, NSA_HD), lambda b, h, ph, nw: (b, h, 0, 0, 0)),
            head,
        ],
        out_specs=head,
    )
    return pl.pallas_call(
        functools.partial(_nsa_decode_attend_body, n_sel=n_sel, kvh=kvh),
        grid_spec=grid_spec,
        out_shape=jax.ShapeDtypeStruct((batch, kvh, NSA_GROUP, NSA_HD), F32),
        compiler_params=_cparams(("parallel", "arbitrary")),
        name="nsa_decode_attend",
    )(phys, isnew, q, *([pool] * n_sel), snew, win, wnew, gate, o_c)


def kernel(x_prompt, x_sample, cache_cmp_kv, cache_sel_kv, state_win_kv, state_hgrn, page_table,
           w_in, w_out, w_up, w_down, cmp_pe, cmp_w, lb_logits, hgrn_norm,
           norm_pre_mix, norm_post_mix, norm_pre_ffn, norm_post_ffn):
    depth = w_in.shape[0]
    bp, tp, d_model = x_prompt.shape
    bs, ts, _ = x_sample.shape
    assert ts == 1, "decode path handles one new token per sequence"
    d_hgrn = hgrn_norm.shape[1]
    heads = d_hgrn // HGRN_DK
    d_nsa = w_out.shape[1] - d_hgrn
    d_ff = w_up.shape[2]
    kvh = d_nsa // NSA_HD // NSA_GROUP
    kv_w = kvh * 2 * NSA_HD
    n_main = 4 * d_hgrn + d_nsa + 3 * kv_w
    assert w_in.shape[2] == n_main + 3 * kvh * NSA_GROUP
    col_q = 4 * d_hgrn
    col_ckv = col_q + d_nsa
    col_skv = col_ckv + kv_w
    col_wkv = col_skv + kv_w
    past_len = page_table.shape[1] * PAGE_SIZE
    wb = state_win_kv.shape[2]
    assert wb == WINDOW and past_len % BLOCK == 0
    nb_p = tp // BLOCK
    nb_past = past_len // BLOCK
    n_blk_s = nb_past + 1

    p_lb = jax.nn.softmax(lb_logits.astype(F32), axis=0)
    lbs = jnp.cumsum(p_lb, axis=0) - p_lb[0:1]
    w_in_t = jnp.swapaxes(w_in, 1, 2)

    def gate_weights(l):
        gate = w_in_t[l, n_main:].reshape(3, kvh, NSA_GROUP, d_model).transpose(1, 0, 2, 3)
        gate = gate.reshape(kvh, 3 * NSA_GROUP, d_model)
        return jnp.pad(gate, ((0, 0), (0, LANES - 3 * NSA_GROUP), (0, 0))).reshape(1, kvh * LANES, d_model)

    def lb_rows(lb):
        lb = lb.reshape(heads, 1, HGRN_DK)
        rows = jnp.concatenate([jnp.log(jnp.maximum(lb, LB_FLOOR)), jnp.log1p(-lb), 1.0 - lb], axis=1)
        return jnp.pad(rows, ((0, 0), (0, 5), (0, 0)))

    kc_len = min(tp, KEY_CHUNK)
    nbp = -(-nb_p // LANES) * LANES
    key_blk = (jnp.arange(tp) // BLOCK).reshape(tp // kc_len, kc_len, 1)
    expand = (jnp.arange(nbp)[None, None, :] == key_blk).astype(BF16)
    pool_cmp = cache_cmp_kv.reshape(depth, cache_cmp_kv.shape[1], PAGE_SIZE * kvh * 2, NSA_HD)
    pool_sel = cache_sel_kv.reshape(depth, cache_sel_kv.shape[1] * (PAGE_SIZE // BLOCK), BLOCK * kvh * 2, NSA_HD)
    win_state = state_win_kv.reshape(depth, bs, wb * kvh * 2, NSA_HD)
    assert tp % PAGE_SIZE == 0
    prompt_pages = jnp.arange(bp * tp // PAGE_SIZE, dtype=jnp.int32).reshape(bp, tp // PAGE_SIZE)

    xp = x_prompt.reshape(bp * tp, d_model)
    xs = x_sample.reshape(bs, d_model)
    hp = _rmsnorm_bf16(xp, norm_pre_mix[0])
    hs = _rmsnorm_bf16(xs, norm_pre_mix[0])
    outs = {k: [] for k in ("p_cmp", "p_sel", "p_win", "p_rec", "s_cmp", "s_sel", "s_win", "s_rec")}
    kv5 = lambda a, b, t: a.reshape(b, t, kvh, 2, NSA_HD)

    for l in range(depth):
        w_gate = gate_weights(l)
        w_c = cmp_w[l].reshape(2, BLOCK * NSA_HD, NSA_HD)
        w_c = jnp.concatenate([w_c[0], w_c[1]], axis=1).astype(BF16)
        pe_c = jnp.tile(cmp_pe[l], (1, kvh, 1))
        lbp = lb_rows(lbs[l])
        gn = hgrn_norm[l].reshape(heads, 1, HGRN_DK)
        nxt = norm_pre_mix[l + 1] if l + 1 < depth else None

        u, us = _matmul(hp, hs, w_in_t, l, n_main, w_t=True)
        ug, ugs = _matmul(hp, hs, w_gate, 0, kvh * LANES, w_t=True)

        o_r, st = _hgrn_prompt(u, lbp, gn, batch=bp, seq=tp, heads=heads, tt=min(tp, 256))
        ckv, skv, wkv = _kv_rows(u, (col_ckv, col_skv, col_wkv), kv_w)
        kc = _compress_paged(ckv.reshape(1, bp * tp // PAGE_SIZE, PAGE_SIZE * kvh * 2, NSA_HD), prompt_pages,
                             pe_c, w_c, layer=0, kvh=kvh)
        kc = kc.reshape(bp, nb_p, kvh, 2, NSA_HD).transpose(3, 0, 2, 1, 4)
        kc = jnp.pad(kc, ((0, 0), (0, 0), (0, 0), (0, nbp - nb_p), (0, 0)))
        o_b = _nsa_prompt(u, ug, kc[0].astype(BF16), jnp.swapaxes(kc[1], 2, 3).astype(BF16), expand,
                          batch=bp, seq=tp, kvh=kvh, col_q=col_q, col_skv=col_skv, col_wkv=col_wkv)
        mix_p = jnp.concatenate([o_r, o_b], axis=1)
        outs["p_cmp"].append(kv5(ckv, bp, tp))
        outs["p_sel"].append(kv5(skv, bp, tp))
        outs["p_win"].append(kv5(wkv, bp, tp)[:, tp - min(WINDOW, tp):])
        outs["p_rec"].append(st)

        o_r, st = _hgrn_decode(us, lbp, gn, state_hgrn[l], batch=bs, heads=heads)
        q_s = us[:, col_q:col_ckv].reshape(bs, kvh, NSA_GROUP, NSA_HD)
        ckv = us[:, col_ckv:col_skv]
        skv = us[:, col_skv:col_wkv]
        wkv = us[:, col_wkv:n_main]
        kc_s = _compress_paged(pool_cmp, page_table, pe_c, w_c, layer=l, kvh=kvh)
        kc_s = kc_s.reshape(bs, nb_past, kvh, 2, NSA_HD).transpose(3, 0, 2, 1, 4)
        o_c, idx = _nsa_decode_select(q_s, kc_s, past_len=past_len, n_blk=n_blk_s)
        idx = idx[..., 0]
        ppos = jnp.minimum(idx, nb_past - 1) * BLOCK
        page = jnp.take_along_axis(page_table, (ppos // PAGE_SIZE).reshape(bs, -1), axis=1)
        phys = page.reshape(idx.shape) * (PAGE_SIZE // BLOCK) + (ppos % PAGE_SIZE) // BLOCK
        isnew = (idx >= nb_past).astype(jnp.int32)
        gate_s = ugs.reshape(bs, kvh, LANES)[:, :, :3 * NSA_GROUP]
        gate_s = jnp.broadcast_to(gate_s.reshape(bs, kvh, 3, NSA_GROUP, 1), (bs, kvh, 3, NSA_GROUP, NSA_HD))
        o_b = _nsa_decode_attend(phys.astype(jnp.int32), isnew, q_s, pool_sel,
                                 skv.reshape(bs, kvh, 1, 2 * NSA_HD), win_state,
                                 wkv.reshape(bs, kvh, 1, 2 * NSA_HD), gate_s, o_c, layer=l)
        mix_s = jnp.concatenate([o_r, o_b.reshape(bs, d_nsa)], axis=1).astype(BF16)
        outs["s_cmp"].append(kv5(ckv, bs, ts))
        outs["s_sel"].append(kv5(skv, bs, ts))
        outs["s_win"].append(jnp.concatenate([state_win_kv[l][:, ts:], kv5(wkv, bs, ts)], axis=1))
        outs["s_rec"].append(st)

        y, ys = _matmul(mix_p, mix_s, w_out, l, d_model)
        xp, fp = _norm_res(y, norm_post_mix[l], xp, norm_pre_ffn[l])
        xs, fs = _norm_res(ys, norm_post_mix[l], xs, norm_pre_ffn[l])
        fp, fs = _matmul(fp, fs, w_up, l, d_ff, act=True, out_dtype=BF16)
        y, ys = _matmul(fp, fs, w_down, l, d_model)
        if nxt is None:
            xp = _norm_res(y, norm_post_ffn[l], xp)
            xs = _norm_res(ys, norm_post_ffn[l], xs)
        else:
            xp, hp = _norm_res(y, norm_post_ffn[l], xp, nxt)
            xs, hs = _norm_res(ys, norm_post_ffn[l], xs, nxt)

    stack = lambda k: jnp.stack(outs[k])
    return (xp.reshape(bp, tp, d_model), xs.reshape(bs, ts, d_model),
            stack("p_cmp"), stack("p_sel"), stack("p_win"), stack("p_rec"),
            stack("s_cmp"), stack("s_sel"), stack("s_win"), stack("s_rec"))
```

```python
import functools
import math

import jax
import jax.numpy as jnp
from jax import lax
from jax.experimental import pallas as pl
from jax.experimental.pallas import tpu as pltpu

F32 = jnp.float32
BF16 = jnp.bfloat16

HGRN_DK = 128
NSA_HD = 128
NSA_GROUP = 4
BLOCK = 64
N_SELECT = 16
WINDOW = 512
PAGE_SIZE = 128
EPS = 1e-6
NEG_INF = -1e30
LB_FLOOR = 1e-30
ATTN_SCALE = NSA_HD ** -0.5
FORCE_SCORE = NSA_GROUP + 1.0

LANES = 128
VMEM_LIMIT = 56 * 1024 * 1024

HGRN_CHUNK = 64
HGRN_SUB = 16
HGRN_HEADS_PER_STEP = 4
PAGES_PER_STEP = 16
KEY_CHUNK = 512
NSA_QUERY_TILE = 256


def _cparams(sem):
    return pltpu.CompilerParams(dimension_semantics=sem, vmem_limit_bytes=VMEM_LIMIT)


def _dot(a, b):
    return jnp.dot(a, b, preferred_element_type=F32)


def _dot_nt(a, b):
    return lax.dot_general(a, b, (((1,), (1,)), ((), ())), preferred_element_type=F32)


def _dot_tn(a, b):
    return lax.dot_general(a, b, (((0,), (0,)), ((), ())), preferred_element_type=F32)


def _sigmoid(x):
    return 1.0 / (1.0 + jnp.exp(-x))


def _silu(x):
    return x * _sigmoid(x)


def _rms(x, g):
    ms = jnp.mean(x * x, axis=-1, keepdims=True)
    return x * lax.rsqrt(ms + EPS) * g


def _masked_softmax(s, mask):
    s = jnp.where(mask, s, NEG_INF)
    e = jnp.where(mask, jnp.exp(s - jnp.max(s, axis=-1, keepdims=True)), 0.0)
    return e / jnp.maximum(jnp.sum(e, axis=-1, keepdims=True), 1e-30)


def _rmsnorm_body(x_ref, g_ref, o_ref):
    o_ref[...] = _rms(x_ref[...], g_ref[...]).astype(o_ref.dtype)


def _rmsnorm_bf16(x, g):
    m, d = x.shape
    tm = min(m, 256)
    return pl.pallas_call(
        _rmsnorm_body,
        grid=(m // tm,),
        in_specs=[pl.BlockSpec((tm, d), lambda i: (i, 0)), pl.BlockSpec((1, d), lambda i: (0, 0))],
        out_specs=pl.BlockSpec((tm, d), lambda i: (i, 0)),
        out_shape=jax.ShapeDtypeStruct((m, d), BF16),
        compiler_params=_cparams(("parallel",)),
        name="rmsnorm",
    )(x, g.reshape(1, d))


def _norm_res_body(y_ref, g1_ref, r_ref, *rest):
    out = r_ref[...] + _rms(y_ref[...], g1_ref[...])
    if len(rest) == 1:
        rest[0][...] = out
    else:
        g2_ref, o_ref, n_ref = rest
        o_ref[...] = out
        n_ref[...] = _rms(out, g2_ref[...]).astype(n_ref.dtype)


def _norm_res(y, g1, res, g2=None):
    m, d = y.shape
    tm = min(m, 256)
    row = pl.BlockSpec((tm, d), lambda i: (i, 0))
    vec = pl.BlockSpec((1, d), lambda i: (0, 0))
    if g2 is None:
        return pl.pallas_call(
            _norm_res_body, grid=(m // tm,), in_specs=[row, vec, row], out_specs=row,
            out_shape=jax.ShapeDtypeStruct((m, d), F32),
            compiler_params=_cparams(("parallel",)), name="norm_res",
        )(y, g1.reshape(1, d), res)
    return pl.pallas_call(
        _norm_res_body, grid=(m // tm,), in_specs=[row, vec, row, vec], out_specs=[row, row],
        out_shape=[jax.ShapeDtypeStruct((m, d), F32), jax.ShapeDtypeStruct((m, d), BF16)],
        compiler_params=_cparams(("parallel",)), name="norm_res_norm",
    )(y, g1.reshape(1, d), res, g2.reshape(1, d))


def _matmul_body(x_ref, xs_ref, w_ref, o_ref, os_ref, *wb_ref, act, nk, w_t):
    i, kk = pl.program_id(1), pl.program_id(2)
    mm = _dot_nt if w_t else _dot

    def finish(part, ref, first):
        if nk == 1:
            if act:
                part = jnp.square(jnp.maximum(part, 0.0))
            ref[...] = part.astype(ref.dtype)
        else:
            @pl.when(first)
            def _():
                ref[...] = part

            @pl.when(jnp.logical_not(first))
            def _():
                ref[...] += part

    if nk == 1:
        @pl.when(i == 0)
        def _():
            wb_ref[0][...] = w_ref[...].astype(BF16)
        wb = wb_ref[0][...]
    else:
        wb = w_ref[...].astype(BF16)

    finish(mm(x_ref[...], wb), o_ref, kk == 0)

    @pl.when(i == 0)
    def _():
        finish(mm(xs_ref[...], wb), os_ref, kk == 0)


def _matmul(x, xs, w, layer, n_out, *, act=False, out_dtype=F32, w_t=False):
    m, k = x.shape
    ms = xs.shape[0]
    tm = min(m, 1024)
    tk, tn = (k, 512) if k <= 4096 else (2048, 1024)
    tn = math.gcd(tn, n_out)
    nk = k // tk
    assert m % tm == 0 and n_out % tn == 0 and k % tk == 0
    assert nk == 1 or (out_dtype == F32 and not act)
    if w_t:
        w_spec = pl.BlockSpec((None, tn, tk), lambda j, i, kk: (layer, j, kk))
        wb_shape = (tn, tk)
    else:
        w_spec = pl.BlockSpec((None, tk, tn), lambda j, i, kk: (layer, kk, j))
        wb_shape = (tk, tn)
    return pl.pallas_call(
        functools.partial(_matmul_body, act=act, nk=nk, w_t=w_t),
        grid=(n_out // tn, m // tm, nk),
        in_specs=[
            pl.BlockSpec((tm, tk), lambda j, i, kk: (i, kk)),
            pl.BlockSpec((ms, tk), lambda j, i, kk: (0, kk)),
            w_spec,
        ],
        out_specs=[
            pl.BlockSpec((tm, tn), lambda j, i, kk: (i, j)),
            pl.BlockSpec((ms, tn), lambda j, i, kk: (0, j)),
        ],
        out_shape=[
            jax.ShapeDtypeStruct((m, n_out), out_dtype),
            jax.ShapeDtypeStruct((ms, n_out), out_dtype),
        ],
        scratch_shapes=[pltpu.VMEM(wb_shape, BF16)] if nk == 1 else [],
        compiler_params=_cparams(("arbitrary", "arbitrary", "arbitrary")),
        name="matmul",
    )(x, xs, w)


def _hgrn_gates(z, lbp):
    a = lbp[0:1, :]
    log_sig = jnp.minimum(z, 0.0) - jnp.log1p(jnp.exp(-jnp.abs(z)))
    b = lbp[1:2, :] + log_sig
    logf = jnp.maximum(a, b) + jnp.log1p(jnp.exp(-jnp.abs(a - b)))
    k = lbp[2:3, :] * _sigmoid(-z)
    return logf, k


def _hgrn_out(o, gn, hg):
    return _rms(o, gn) * _silu(hg)


def _hgrn_chunk(q, z, v, hg, lbp, gn, st, consts):
    c, sub = HGRN_CHUNK, HGRN_SUB
    ns = c // sub
    tri, ones, t_in_sub = consts
    q = _silu(q)
    logf, k = _hgrn_gates(z, lbp)

    hi = logf.astype(BF16)
    r1 = logf - hi.astype(F32)
    mid = r1.astype(BF16)
    lo = (r1 - mid.astype(F32)).astype(BF16)
    cb = _dot(tri, hi) + _dot(tri, mid) + _dot(tri, lo)

    q3 = q.reshape(ns, sub, HGRN_DK)
    k3 = k.reshape(ns, sub, HGRN_DK)
    v3 = v.reshape(ns, sub, HGRN_DK)
    cb3 = cb.reshape(ns, sub, HGRN_DK)
    e_end = cb3[:, sub - 1:sub, :]
    b_beg = jnp.concatenate([jnp.zeros((1, 1, HGRN_DK), F32), e_end[:ns - 1]], axis=0)
    qt = q3 * jnp.exp(cb3 - b_beg)
    kt = k3 * jnp.exp(e_end - cb3)

    q_cols, k_cols = [], []
    zeros_sub = jnp.zeros((sub, HGRN_DK), F32)
    for j in range(ns - 1):
        qc, kc = [], []
        for i in range(ns):
            qc.append(qt[i] * jnp.exp(b_beg[i] - e_end[j]) if i > j else zeros_sub)
            kc.append(kt[i] if i == j else zeros_sub)
        q_cols.append(jnp.concatenate(qc, axis=0))
        k_cols.append(jnp.concatenate(kc, axis=0))
    q_hat = jnp.concatenate(q_cols, axis=1).astype(BF16)
    k_hat = jnp.concatenate(k_cols, axis=1).astype(BF16)
    att_off = _dot_nt(q_hat, k_hat)
    o = _dot(att_off.astype(BF16), v.astype(BF16))

    parts = []
    for s in range(sub):
        dec = jnp.exp(cb3 - cb3[:, s:s + 1, :])
        p = jnp.where(t_in_sub >= s, q3 * k3[:, s:s + 1, :] * dec, 0.0)
        parts.append(p.reshape(c, HGRN_DK).astype(BF16))
    rsum = _dot(jnp.concatenate(parts, axis=0), ones)
    o_diag = jnp.zeros((ns, sub, HGRN_DK), F32)
    for s in range(sub):
        o_diag = o_diag + rsum[s * c:(s + 1) * c, :].reshape(ns, sub, HGRN_DK) * v3[:, s:s + 1, :]
    o = o + o_diag.reshape(c, HGRN_DK)

    o = o + _dot_nt((q * jnp.exp(cb)).astype(BF16), st.astype(BF16))
    last = cb[c - 1:c, :]
    kd = (k * jnp.exp(last - cb)).astype(BF16)
    st_new = st * jnp.exp(last) + _dot_tn(v.astype(BF16), kd)
    return _hgrn_out(o, gn, hg), st_new


def _hgrn_prompt_body(q_ref, f_ref, i_ref, g_ref, lb_ref, gn_ref, o_ref, st_ref, s_ref, *, n_chunks):
    c, sub = HGRN_CHUNK, HGRN_SUB
    hp = s_ref.shape[0]
    tstep = pl.program_id(2)

    @pl.when(tstep == 0)
    def _():
        s_ref[...] = jnp.zeros_like(s_ref)

    row = lax.broadcasted_iota(jnp.int32, (c, c), 0)
    col = lax.broadcasted_iota(jnp.int32, (c, c), 1)
    consts = (
        jnp.where(row >= col, 1.0, 0.0).astype(BF16),
        jnp.ones((LANES, LANES), BF16),
        lax.broadcasted_iota(jnp.int32, (c // sub, sub, HGRN_DK), 1),
    )

    def chunk(ci, carry):
        rows = pl.ds(pl.multiple_of(ci * c, c), c)
        for hh in range(hp):
            cols = slice(hh * HGRN_DK, (hh + 1) * HGRN_DK)
            out, st_new = _hgrn_chunk(q_ref[rows, cols], f_ref[rows, cols], i_ref[rows, cols],
                                      g_ref[rows, cols], lb_ref[hh], gn_ref[hh], s_ref[hh], consts)
            s_ref[hh] = st_new
            o_ref[rows, cols] = out.astype(o_ref.dtype)
        return carry

    lax.fori_loop(0, n_chunks, chunk, 0)

    @pl.when(tstep == pl.num_programs(2) - 1)
    def _():
        for hh in range(hp):
            st_ref[hh] = s_ref[hh].T


def _hgrn_prompt(u, lbp, gn, *, batch, seq, heads, tt):
    nt = seq // tt
    hp = math.gcd(heads, HGRN_HEADS_PER_STEP)
    w = hp * HGRN_DK
    ng = heads // hp
    blk = lambda part: pl.BlockSpec((tt, w), lambda b, h, t: (b * nt + t, part * ng + h))
    return pl.pallas_call(
        functools.partial(_hgrn_prompt_body, n_chunks=tt // HGRN_CHUNK),
        grid=(batch, ng, nt),
        in_specs=[
            blk(0), blk(1), blk(2), blk(3),
            pl.BlockSpec((hp, 8, HGRN_DK), lambda b, h, t: (h, 0, 0)),
            pl.BlockSpec((hp, 1, HGRN_DK), lambda b, h, t: (h, 0, 0)),
        ],
        out_specs=[
            pl.BlockSpec((tt, w), lambda b, h, t: (b * nt + t, h)),
            pl.BlockSpec((None, hp, HGRN_DK, HGRN_DK), lambda b, h, t: (b, h, 0, 0)),
        ],
        out_shape=[
            jax.ShapeDtypeStruct((batch * seq, heads * HGRN_DK), BF16),
            jax.ShapeDtypeStruct((batch, heads, HGRN_DK, HGRN_DK), F32),
        ],
        scratch_shapes=[pltpu.VMEM((hp, HGRN_DK, HGRN_DK), F32)],
        compiler_params=_cparams(("parallel", "parallel", "arbitrary")),
        name="hgrn_prompt",
    )(u, u, u, u, lbp, gn)


def _hgrn_decode_body(q_ref, f_ref, i_ref, g_ref, lb_ref, gn_ref, s0_ref, o_ref, st_ref):
    logf, k = _hgrn_gates(f_ref[...], lb_ref[...])
    decay = jnp.exp(logf)
    q = _silu(q_ref[...])
    v = i_ref[...]

    def col(x):
        return jnp.broadcast_to(x, (HGRN_DK, HGRN_DK)).T

    outs = []
    for b in range(s0_ref.shape[0]):
        r = slice(b, b + 1)
        s_new = col(decay[r]) * s0_ref[b] + col(k[r]) * v[r]
        st_ref[b] = s_new
        outs.append(jnp.sum(col(q[r]) * s_new, axis=0, keepdims=True))
    o_ref[...] = _hgrn_out(jnp.concatenate(outs, axis=0), gn_ref[...], g_ref[...])


def _hgrn_decode(u, lbp, gn, s0, *, layer, batch, heads):
    blk = lambda off: pl.BlockSpec((batch, HGRN_DK), lambda h: (0, off + h))
    return pl.pallas_call(
        _hgrn_decode_body,
        grid=(heads,),
        in_specs=[
            blk(0), blk(heads), blk(2 * heads), blk(3 * heads),
            pl.BlockSpec((None, 8, HGRN_DK), lambda h: (h, 0, 0)),
            pl.BlockSpec((None, 1, HGRN_DK), lambda h: (h, 0, 0)),
            pl.BlockSpec((None, batch, None, HGRN_DK, HGRN_DK), lambda h: (layer, 0, h, 0, 0)),
        ],
        out_specs=[
            pl.BlockSpec((batch, HGRN_DK), lambda h: (0, h)),
            pl.BlockSpec((batch, None, HGRN_DK, HGRN_DK), lambda h: (0, h, 0, 0)),
        ],
        out_shape=[
            jax.ShapeDtypeStruct((batch, heads * HGRN_DK), F32),
            jax.ShapeDtypeStruct(s0.shape[1:], F32),
        ],
        compiler_params=_cparams(("parallel",)),
        name="hgrn_decode",
    )(u, u, u, u, lbp, gn, s0)


def _kv_rows_body(*refs, hc):
    n = len(refs) // 2
    for x_ref, o_ref in zip(refs[:n], refs[n:]):
        tm = x_ref.shape[0]
        for j in range(hc):
            o_ref[pl.ds(j, tm, stride=hc), :] = x_ref[:, j * NSA_HD:(j + 1) * NSA_HD]


def _kv_rows(u, cols, kv_w):
    m = u.shape[0]
    tm = min(m, 256)
    hc = kv_w // NSA_HD
    assert all(c % kv_w == 0 for c in cols)
    return pl.pallas_call(
        functools.partial(_kv_rows_body, hc=hc),
        grid=(m // tm,),
        in_specs=[pl.BlockSpec((tm, kv_w), lambda i, c=c: (i, c // kv_w)) for c in cols],
        out_specs=[pl.BlockSpec((tm * hc, NSA_HD), lambda i: (i, 0)) for _ in cols],
        out_shape=[jax.ShapeDtypeStruct((m * hc, NSA_HD), F32) for _ in cols],
        compiler_params=_cparams(("parallel",)),
        name="kv_rows",
    )(*([u] * len(cols)))


def _compress_paged_body(pt_ref, *refs, kvh, npg):
    del pt_ref
    page_refs = refs[:npg]
    pe_ref, w_ref, o_ref, lhs_ref = refs[npg:]
    hc = 2 * kvh
    bpp = PAGE_SIZE // BLOCK
    nblk = npg * bpp
    for r in range(BLOCK):
        rows = [page_refs[p][(half * BLOCK + r) * hc:(half * BLOCK + r + 1) * hc, :]
                for p in range(npg) for half in range(bpp)]
        x = jnp.concatenate(rows, axis=0).reshape(nblk, hc, NSA_HD) + pe_ref[r]
        lhs_ref[:, r * NSA_HD:(r + 1) * NSA_HD] = x.reshape(nblk * hc, NSA_HD).astype(BF16)
    y = _dot(lhs_ref[...], w_ref[...])
    is_v = lax.broadcasted_iota(jnp.int32, (nblk * hc, NSA_HD), 0) % 2 == 1
    out = jnp.where(is_v, y[:, NSA_HD:], y[:, :NSA_HD])
    o_ref[...] = out.reshape(nblk, hc, NSA_HD)


def _compress_paged(pool, page_table, pe, w, *, layer, kvh):
    batch, n_pages = page_table.shape
    npg = math.gcd(n_pages, PAGES_PER_STEP)
    rpp = pool.shape[2]
    hc = 2 * kvh
    nblk = npg * PAGE_SIZE // BLOCK

    def page_spec(p):
        return pl.BlockSpec((None, None, rpp, NSA_HD),
                            lambda b, g, pt: (layer, pt[b, g * npg + p], 0, 0))

    grid_spec = pltpu.PrefetchScalarGridSpec(
        num_scalar_prefetch=1,
        grid=(batch, n_pages // npg),
        in_specs=[page_spec(p) for p in range(npg)] + [
            pl.BlockSpec((BLOCK, hc, NSA_HD), lambda b, g, pt: (0, 0, 0)),
            pl.BlockSpec((BLOCK * NSA_HD, 2 * NSA_HD), lambda b, g, pt: (0, 0)),
        ],
        out_specs=pl.BlockSpec((None, nblk, hc, NSA_HD), lambda b, g, pt: (b, g, 0, 0)),
        scratch_shapes=[pltpu.VMEM((nblk * hc, BLOCK * NSA_HD), BF16)],
    )
    return pl.pallas_call(
        functools.partial(_compress_paged_body, kvh=kvh, npg=npg),
        grid_spec=grid_spec,
        out_shape=jax.ShapeDtypeStruct((batch, n_pages * PAGE_SIZE // BLOCK, hc, NSA_HD), F32),
        compiler_params=_cparams(("parallel", "arbitrary")),
        name="compress_paged",
    )(page_table, *([pool] * npg), pe, w)


def _nsa_prompt_body(q_ref, kc_ref, vct_ref, skv_ref, wkv_ref, gate_ref, e_ref, o_ref,
                     ks_ref, vst_ref, kw_ref, vwt_ref, m_ref, l_ref, acc_ref, *, tq, seq, n_blk, span, kc_len):
    i = pl.program_id(2)

    @pl.when(i == 0)
    def _():
        ks_ref[...] = skv_ref[:, :NSA_HD].astype(BF16)
        kw_ref[...] = wkv_ref[:, :NSA_HD].astype(BF16)
        for c in range(seq // kc_len):
            vst_ref[c] = skv_ref[c * kc_len:(c + 1) * kc_len, NSA_HD:].T.astype(BF16)
        for c in range(seq // tq):
            vwt_ref[c] = wkv_ref[c * tq:(c + 1) * tq, NSA_HD:].T.astype(BF16)

    nbp = kc_ref.shape[0]
    nb8 = -(-n_blk // 8) * 8
    n_sel = min(N_SELECT, n_blk)
    q0 = i * tq
    qpos = q0 + lax.broadcasted_iota(jnp.int32, (1, tq), 1)
    cur = qpos // BLOCK
    n_sub = lax.broadcasted_iota(jnp.int32, (nbp, 1), 0)
    cmask = (n_sub * BLOCK + (BLOCK - 1) <= qpos) & (n_sub < seq // BLOCK)

    kc = kc_ref[...]
    vct = vct_ref[...]
    q_all = jnp.concatenate(
        [(q_ref[:, g * NSA_HD:(g + 1) * NSA_HD] * ATTN_SCALE).astype(BF16) for g in range(NSA_GROUP)], axis=0)
    per_head = lambda a: jnp.concatenate([a] * NSA_GROUP, axis=1)
    head = lambda a, g: a[:, g * tq:(g + 1) * tq]

    cmask4 = per_head(cmask)
    s = jnp.where(cmask4, _dot_nt(kc, q_all), NEG_INF)
    e = jnp.where(cmask4, jnp.exp(s - jnp.max(s, axis=0, keepdims=True)), 0.0)
    p = e / jnp.maximum(jnp.sum(e, axis=0, keepdims=True), 1e-30)
    o_c = _dot(vct, p.astype(BF16))
    imp = head(p, 0)
    for g in range(1, NSA_GROUP):
        imp = imp + head(p, g)

    forced = (n_sub == 0) | (n_sub == cur) | (n_sub == cur - 1)
    score = jnp.where(n_sub <= cur, jnp.where(forced, FORCE_SCORE, imp), -1.0)[:nb8, :]
    j_idx = lax.broadcasted_iota(jnp.int32, (nb8, tq), 0)
    score = jnp.where(j_idx < n_blk, score, -2.0)
    cnt = jnp.zeros((nb8, tq), F32)
    for j in range(n_blk):
        rj = score[j:j + 1, :]
        beats = jnp.where(rj > score, 1.0, jnp.where((rj == score) & (j_idx > j), 1.0, 0.0))
        cnt = cnt + beats
    sel = jnp.where((cnt < n_sel) & (j_idx <= cur) & (j_idx < n_blk), 1.0, 0.0)
    if nbp > nb8:
        sel = jnp.concatenate([sel, jnp.zeros((nbp - nb8, tq), F32)], axis=0)
    sel = sel.astype(BF16)

    m_ref[...] = jnp.full_like(m_ref, NEG_INF)
    l_ref[...] = jnp.zeros_like(l_ref)
    acc_ref[...] = jnp.zeros_like(acc_ref)

    def key_chunk(ci, carry):
        k0 = pl.multiple_of(ci * kc_len, kc_len)
        kk = ks_ref[pl.ds(k0, kc_len), :]
        vt = vst_ref[ci]
        kpos = k0 + lax.broadcasted_iota(jnp.int32, (kc_len, 1), 0)
        bias = jnp.where((_dot(e_ref[ci], sel) > 0.5) & (kpos <= qpos), 0.0, NEG_INF)
        s = _dot_nt(kk, q_all) + per_head(bias)
        m_old = m_ref[...]
        m_new = jnp.maximum(m_old, jnp.max(s, axis=0, keepdims=True))
        alpha = jnp.exp(m_old - m_new)
        p = jnp.exp(s - m_new)
        l_ref[...] = alpha * l_ref[...] + jnp.sum(p, axis=0, keepdims=True)
        acc_ref[...] = alpha * acc_ref[...] + _dot(vt, p.astype(BF16))
        m_ref[...] = m_new
        return carry

    lax.fori_loop(0, (q0 + tq + kc_len - 1) // kc_len, key_chunk, 0)

    nwb = span // tq
    blk0 = jnp.maximum(i + 1 - nwb, 0)
    start = pl.multiple_of(blk0 * tq, tq)
    wpos = start + lax.broadcasted_iota(jnp.int32, (span, 1), 0)
    dist = qpos - wpos
    wbias = jnp.where((dist >= 0) & (dist < WINDOW), 0.0, NEG_INF)
    kw = kw_ref[pl.ds(start, span), :]
    vwt = jnp.concatenate([vwt_ref[blk0 + j] for j in range(nwb)], axis=1)

    o_s = acc_ref[...] / l_ref[...]
    s = _dot_nt(kw, q_all) + per_head(wbias)
    e = jnp.exp(s - jnp.max(s, axis=0, keepdims=True))
    o_w = _dot(vwt, e.astype(BF16)) / jnp.sum(e, axis=0, keepdims=True)

    gates = _sigmoid(gate_ref[...]).T
    for g in range(NSA_GROUP):
        gc = gates[g:g + 1, :]
        gs = gates[NSA_GROUP + g:NSA_GROUP + g + 1, :]
        gw = gates[2 * NSA_GROUP + g:2 * NSA_GROUP + g + 1, :]
        o = gc * head(o_c, g) + gs * head(o_s, g) + gw * head(o_w, g)
        o_ref[:, g * NSA_HD:(g + 1) * NSA_HD] = o.T.astype(o_ref.dtype)


def _nsa_prompt(u, ug, kc, vct, expand, *, batch, seq, kvh, col_q, col_skv, col_wkv):
    tq = min(seq, NSA_QUERY_TILE)
    assert tq % LANES == 0 and seq % tq == 0
    nq = seq // tq
    span = min(seq, WINDOW + tq)
    qw = NSA_GROUP * NSA_HD
    nbp = kc.shape[2]
    n_kc, kc_len, _ = expand.shape
    return pl.pallas_call(
        functools.partial(_nsa_prompt_body, tq=tq, seq=seq, n_blk=seq // BLOCK, span=span, kc_len=kc_len),
        grid=(batch, kvh, nq),
        in_specs=[
            pl.BlockSpec((tq, qw), lambda b, h, i: (b * nq + i, col_q // qw + h)),
            pl.BlockSpec((None, None, nbp, NSA_HD), lambda b, h, i: (b, h, 0, 0)),
            pl.BlockSpec((None, None, NSA_HD, nbp), lambda b, h, i: (b, h, 0, 0)),
            pl.BlockSpec((seq, 2 * NSA_HD), lambda b, h, i: (b, col_skv // (2 * NSA_HD) + h)),
            pl.BlockSpec((seq, 2 * NSA_HD), lambda b, h, i: (b, col_wkv // (2 * NSA_HD) + h)),
            pl.BlockSpec((tq, LANES), lambda b, h, i: (b * nq + i, h)),
            pl.BlockSpec((n_kc, kc_len, nbp), lambda b, h, i: (0, 0, 0)),
        ],
        out_specs=pl.BlockSpec((tq, qw), lambda b, h, i: (b * nq + i, h)),
        out_shape=jax.ShapeDtypeStruct((batch * seq, kvh * qw), BF16),
        scratch_shapes=[
            pltpu.VMEM((seq, NSA_HD), BF16),
            pltpu.VMEM((n_kc, NSA_HD, kc_len), BF16),
            pltpu.VMEM((seq, NSA_HD), BF16),
            pltpu.VMEM((nq, NSA_HD, tq), BF16),
            pltpu.VMEM((1, NSA_GROUP * tq), F32),
            pltpu.VMEM((1, NSA_GROUP * tq), F32),
            pltpu.VMEM((NSA_HD, NSA_GROUP * tq), F32),
        ],
        compiler_params=_cparams(("parallel", "parallel", "arbitrary")),
        name="nsa_prompt",
    )(u, kc, vct, u, u, ug, expand)


def _nsa_decode_select_body(q_ref, kc_ref, oc_ref, idx_ref, *, past_len, n_blk):
    nbc = kc_ref.shape[1]
    npad = -(-n_blk // LANES) * LANES
    n_sel = min(N_SELECT, n_blk)
    qpos = past_len
    cur = qpos // BLOCK
    q = q_ref[...].astype(BF16)
    n_lane = lax.broadcasted_iota(jnp.int32, (1, nbc), 1)
    cmask = n_lane * BLOCK + (BLOCK - 1) <= qpos
    p = _masked_softmax(_dot_nt(q, kc_ref[0].astype(BF16)) * ATTN_SCALE, cmask)
    oc_ref[...] = _dot(p.astype(BF16), kc_ref[1].astype(BF16))
    imp = jnp.sum(p, axis=0, keepdims=True)
    imp = jnp.concatenate([imp, jnp.zeros((1, npad - nbc), F32)], axis=1)

    j_lane = lax.broadcasted_iota(jnp.int32, (1, npad), 1)
    forced = (j_lane == 0) | (j_lane == cur) | (j_lane == cur - 1)
    score = jnp.where(j_lane <= cur, jnp.where(forced, FORCE_SCORE, imp), -1.0)
    score = jnp.where(j_lane < n_blk, score, -2.0)
    s_row = jnp.broadcast_to(score, (npad, npad))
    s_col = s_row.T
    jp = lax.broadcasted_iota(jnp.int32, (npad, npad), 0)
    jj = lax.broadcasted_iota(jnp.int32, (npad, npad), 1)
    beats = jnp.where(s_col > s_row, 1.0, jnp.where((s_col == s_row) & (jp < jj), 1.0, 0.0))
    rank = jnp.sum(beats, axis=0, keepdims=True)
    k_sub = lax.broadcasted_iota(jnp.int32, (N_SELECT, npad), 0).astype(F32)
    j_f = lax.broadcasted_iota(jnp.int32, (N_SELECT, npad), 1).astype(F32)
    hit = jnp.where((rank == k_sub) & (j_f < n_blk), j_f, 0.0)
    idx = jnp.sum(hit, axis=1, keepdims=True)
    idx = jnp.where(k_sub[:, :1] < n_sel, idx, 0.0)
    idx_ref[...] = jnp.broadcast_to(idx, (N_SELECT, LANES)).astype(jnp.int32)


def _nsa_decode_select(q, kc, *, past_len, n_blk):
    batch, kvh = q.shape[:2]
    nbc = kc.shape[3]
    return pl.pallas_call(
        functools.partial(_nsa_decode_select_body, past_len=past_len, n_blk=n_blk),
        grid=(batch, kvh),
        in_specs=[
            pl.BlockSpec((None, None, NSA_GROUP, NSA_HD), lambda b, h: (b, h, 0, 0)),
            pl.BlockSpec((2, None, None, nbc, NSA_HD), lambda b, h: (0, b, h, 0, 0)),
        ],
        out_specs=[
            pl.BlockSpec((None, None, NSA_GROUP, NSA_HD), lambda b, h: (b, h, 0, 0)),
            pl.BlockSpec((None, None, N_SELECT, LANES), lambda b, h: (b, h, 0, 0)),
        ],
        out_shape=[
            jax.ShapeDtypeStruct((batch, kvh, NSA_GROUP, NSA_HD), F32),
            jax.ShapeDtypeStruct((batch, kvh, N_SELECT, LANES), jnp.int32),
        ],
        compiler_params=_cparams(("parallel", "parallel")),
        name="nsa_decode_select",
    )(q, kc)


def _nsa_decode_attend_body(phys_ref, isnew_ref, q_ref, *refs, n_sel, kvh):
    b, h = pl.program_id(0), pl.program_id(1)
    del phys_ref
    blk_refs = refs[:n_sel]
    snew_ref, win_ref, wnew_ref, gate_ref, oc_ref, o_ref = refs[n_sel:]
    q = q_ref[...]
    qb = q.astype(BF16)
    rstride = 2 * kvh

    def attend(kb, vb, valid, k_new, v_new, use_new):
        s = jnp.where(valid, _dot_nt(qb, kb) * ATTN_SCALE, NEG_INF)
        s_n = jnp.where(use_new, jnp.sum(q * k_new, axis=-1, keepdims=True) * ATTN_SCALE, NEG_INF)
        mx = jnp.maximum(jnp.max(s, axis=-1, keepdims=True), s_n)
        e = jnp.where(valid, jnp.exp(s - mx), 0.0)
        e_n = jnp.where(use_new, jnp.exp(s_n - mx), 0.0)
        den = jnp.maximum(jnp.sum(e, axis=-1, keepdims=True) + e_n, 1e-30)
        return (_dot(e.astype(BF16), vb) + e_n * v_new) / den

    lane_blk = lax.broadcasted_iota(jnp.int32, (1, n_sel * BLOCK), 1) // BLOCK
    valid = jnp.zeros((1, n_sel * BLOCK), jnp.int32)
    any_new = jnp.int32(0)
    for k in range(n_sel):
        is_new = isnew_ref[b, h, k]
        valid = jnp.where(lane_blk == k, 1 - is_new, valid)
        any_new = jnp.maximum(any_new, is_new)
    kb = jnp.concatenate([r[pl.ds(2 * h, BLOCK, stride=rstride), :] for r in blk_refs], axis=0)
    vb = jnp.concatenate([r[pl.ds(2 * h + 1, BLOCK, stride=rstride), :] for r in blk_refs], axis=0)
    o_s = attend(kb.astype(BF16), vb.astype(BF16), valid > 0,
                 snew_ref[:, :NSA_HD], snew_ref[:, NSA_HD:], any_new > 0)

    wb = win_ref.shape[0] // rstride
    kw = win_ref[pl.ds(2 * h, wb, stride=rstride), :].astype(BF16)
    vw = win_ref[pl.ds(2 * h + 1, wb, stride=rstride), :].astype(BF16)
    prow = lax.broadcasted_iota(jnp.int32, (1, wb), 1)
    o_w = attend(kw, vw, prow >= 1, wnew_ref[:, :NSA_HD], wnew_ref[:, NSA_HD:], True)

    gates = _sigmoid(gate_ref[...])
    o_ref[...] = gates[0] * oc_ref[...] + gates[1] * o_s + gates[2] * o_w


def _nsa_decode_attend(phys, isnew, q, pool, snew, win, wnew, gate, o_c, *, layer):
    batch, kvh, n_sel = phys.shape
    kv2 = 2 * NSA_HD
    brows = pool.shape[2]
    wrows = win.shape[2]
    def blk_spec(k):
        return pl.BlockSpec((None, None, brows, NSA_HD), lambda b, h, ph, nw: (layer, ph[b, h, k], 0, 0))

    head = pl.BlockSpec((None, None, NSA_GROUP, NSA_HD), lambda b, h, ph, nw: (b, h, 0, 0))
    new_row = pl.BlockSpec((None, None, 1, kv2), lambda b, h, ph, nw: (b, h, 0, 0))
    grid_spec = pltpu.PrefetchScalarGridSpec(
        num_scalar_prefetch=2,
        grid=(batch, kvh),
        in_specs=[head] + [blk_spec(k) for k in range(n_sel)] + [
            new_row,
            pl.BlockSpec((None, None, wrows, NSA_HD), lambda b, h, ph, nw: (layer, b, 0, 0)),
            new_row,
            pl.BlockSpec((None, None, 3, NSA_GROUP, NSA_HD), lambda b, h, ph, nw: (b, h, 0, 0, 0)),
            head,
        ],
        out_specs=head,
    )
    return pl.pallas_call(
        functools.partial(_nsa_decode_attend_body, n_sel=n_sel, kvh=kvh),
        grid_spec=grid_spec,
        out_shape=jax.ShapeDtypeStruct((batch, kvh, NSA_GROUP, NSA_HD), F32),
        compiler_params=_cparams(("parallel", "arbitrary")),
        name="nsa_decode_attend",
    )(phys, isnew, q, *([pool] * n_sel), snew, win, wnew, gate, o_c)


def kernel(x_prompt, x_sample, cache_cmp_kv, cache_sel_kv, state_win_kv, state_hgrn, page_table,
           w_in, w_out, w_up, w_down, cmp_pe, cmp_w, lb_logits, hgrn_norm,
           norm_pre_mix, norm_post_mix, norm_pre_ffn, norm_post_ffn):
    depth = w_in.shape[0]
    bp, tp, d_model = x_prompt.shape
    bs, ts, _ = x_sample.shape
    assert ts == 1, "decode path handles one new token per sequence"
    d_hgrn = hgrn_norm.shape[1]
    heads = d_hgrn // HGRN_DK
    d_nsa = w_out.shape[1] - d_hgrn
    d_ff = w_up.shape[2]
    kvh = d_nsa // NSA_HD // NSA_GROUP
    kv_w = kvh * 2 * NSA_HD
    n_main = 4 * d_hgrn + d_nsa + 3 * kv_w
    assert w_in.shape[2] == n_main + 3 * kvh * NSA_GROUP
    col_q = 4 * d_hgrn
    col_ckv = col_q + d_nsa
    col_skv = col_ckv + kv_w
    col_wkv = col_skv + kv_w
    past_len = page_table.shape[1] * PAGE_SIZE
    wb = state_win_kv.shape[2]
    assert wb == WINDOW and past_len % BLOCK == 0
    nb_p = tp // BLOCK
    nb_past = past_len // BLOCK
    n_blk_s = nb_past + 1

    p_lb = jax.nn.softmax(lb_logits.astype(F32), axis=0)
    lbs = jnp.cumsum(p_lb, axis=0) - p_lb[0:1]
    w_in_t = jnp.swapaxes(w_in, 1, 2)

    def gate_weights(l):
        gate = w_in_t[l, n_main:].reshape(3, kvh, NSA_GROUP, d_model).transpose(1, 0, 2, 3)
        gate = gate.reshape(kvh, 3 * NSA_GROUP, d_model)
        return jnp.pad(gate, ((0, 0), (0, LANES - 3 * NSA_GROUP), (0, 0))).reshape(1, kvh * LANES, d_model)

    def lb_rows(lb):
        lb = lb.reshape(heads, 1, HGRN_DK)
        rows = jnp.concatenate([jnp.log(jnp.maximum(lb, LB_FLOOR)), jnp.log1p(-lb), 1.0 - lb], axis=1)
        return jnp.pad(rows, ((0, 0), (0, 5), (0, 0)))

    kc_len = min(tp, KEY_CHUNK)
    nbp = -(-nb_p // LANES) * LANES
    key_blk = (jnp.arange(tp) // BLOCK).reshape(tp // kc_len, kc_len, 1)
    expand = (jnp.arange(nbp)[None, None, :] == key_blk).astype(BF16)
    pool_cmp = cache_cmp_kv.reshape(depth, cache_cmp_kv.shape[1], PAGE_SIZE * kvh * 2, NSA_HD)
    pool_sel = cache_sel_kv.reshape(depth, cache_sel_kv.shape[1] * (PAGE_SIZE // BLOCK), BLOCK * kvh * 2, NSA_HD)
    win_state = state_win_kv.reshape(depth, bs, wb * kvh * 2, NSA_HD)
    assert tp % PAGE_SIZE == 0
    prompt_pages = jnp.arange(bp * tp // PAGE_SIZE, dtype=jnp.int32).reshape(bp, tp // PAGE_SIZE)

    xp = x_prompt.reshape(bp * tp, d_model)
    xs = x_sample.reshape(bs, d_model)
    hp = _rmsnorm_bf16(xp, norm_pre_mix[0])
    hs = _rmsnorm_bf16(xs, norm_pre_mix[0])
    outs = {k: [] for k in ("p_cmp", "p_sel", "p_win", "p_rec", "s_cmp", "s_sel", "s_win", "s_rec")}
    kv5 = lambda a, b, t: a.reshape(b, t, kvh, 2, NSA_HD)

    for l in range(depth):
        w_gate = gate_weights(l)
        w_c = cmp_w[l].reshape(2, BLOCK * NSA_HD, NSA_HD)
        w_c = jnp.concatenate([w_c[0], w_c[1]], axis=1).astype(BF16)
        pe_c = jnp.tile(cmp_pe[l], (1, kvh, 1))
        lbp = lb_rows(lbs[l])
        gn = hgrn_norm[l].reshape(heads, 1, HGRN_DK)
        nxt = norm_pre_mix[l + 1] if l + 1 < depth else None

        u, us = _matmul(hp, hs, w_in_t, l, n_main, w_t=True)
        ug, ugs = _matmul(hp, hs, w_gate, 0, kvh * LANES, w_t=True)

        o_r, st = _hgrn_prompt(u, lbp, gn, batch=bp, seq=tp, heads=heads, tt=min(tp, 256))
        ckv, skv, wkv = _kv_rows(u, (col_ckv, col_skv, col_wkv), kv_w)
        kc = _compress_paged(ckv.reshape(1, bp * tp // PAGE_SIZE, PAGE_SIZE * kvh * 2, NSA_HD), prompt_pages,
                             pe_c, w_c, layer=0, kvh=kvh)
        kc = kc.reshape(bp, nb_p, kvh, 2, NSA_HD).transpose(3, 0, 2, 1, 4)
        kc = jnp.pad(kc, ((0, 0), (0, 0), (0, 0), (0, nbp - nb_p), (0, 0)))
        o_b = _nsa_prompt(u, ug, kc[0].astype(BF16), jnp.swapaxes(kc[1], 2, 3).astype(BF16), expand,
                          batch=bp, seq=tp, kvh=kvh, col_q=col_q, col_skv=col_skv, col_wkv=col_wkv)
        mix_p = jnp.concatenate([o_r, o_b], axis=1)
        outs["p_cmp"].append(kv5(ckv, bp, tp))
        outs["p_sel"].append(kv5(skv, bp, tp))
        outs["p_win"].append(kv5(wkv, bp, tp)[:, tp - min(WINDOW, tp):])
        outs["p_rec"].append(st)

        o_r, st = _hgrn_decode(us, lbp, gn, state_hgrn, layer=l, batch=bs, heads=heads)
        q_s = us[:, col_q:col_ckv].reshape(bs, kvh, NSA_GROUP, NSA_HD)
        ckv = us[:, col_ckv:col_skv]
        skv = us[:, col_skv:col_wkv]
        wkv = us[:, col_wkv:n_main]
        kc_s = _compress_paged(pool_cmp, page_table, pe_c, w_c, layer=l, kvh=kvh)
        kc_s = kc_s.reshape(bs, nb_past, kvh, 2, NSA_HD).transpose(3, 0, 2, 1, 4)
        o_c, idx = _nsa_decode_select(q_s, kc_s, past_len=past_len, n_blk=n_blk_s)
        idx = idx[..., 0]
        ppos = jnp.minimum(idx, nb_past - 1) * BLOCK
        page = jnp.take_along_axis(page_table, (ppos // PAGE_SIZE).reshape(bs, -1), axis=1)
        phys = page.reshape(idx.shape) * (PAGE_SIZE // BLOCK) + (ppos % PAGE_SIZE) // BLOCK
        isnew = (idx >= nb_past).astype(jnp.int32)
        gate_s = ugs.reshape(bs, kvh, LANES)[:, :, :3 * NSA_GROUP]
        gate_s = jnp.broadcast_to(gate_s.reshape(bs, kvh, 3, NSA_GROUP, 1), (bs, kvh, 3, NSA_GROUP, NSA_HD))
        o_b = _nsa_decode_attend(phys.astype(jnp.int32), isnew, q_s, pool_sel,
                                 skv.reshape(bs, kvh, 1, 2 * NSA_HD), win_state,
                                 wkv.reshape(bs, kvh, 1, 2 * NSA_HD), gate_s, o_c, layer=l)
        mix_s = jnp.concatenate([o_r, o_b.reshape(bs, d_nsa)], axis=1).astype(BF16)
        outs["s_cmp"].append(kv5(ckv, bs, ts))
        outs["s_sel"].append(kv5(skv, bs, ts))
        outs["s_win"].append(jnp.concatenate([state_win_kv[l][:, ts:], kv5(wkv, bs, ts)], axis=1))
        outs["s_rec"].append(st)

        y, ys = _matmul(mix_p, mix_s, w_out, l, d_model)
        xp, fp = _norm_res(y, norm_post_mix[l], xp, norm_pre_ffn[l])
        xs, fs = _norm_res(ys, norm_post_mix[l], xs, norm_pre_ffn[l])
        fp, fs = _matmul(fp, fs, w_up, l, d_ff, act=True, out_dtype=BF16)
        y, ys = _matmul(fp, fs, w_down, l, d_model)
        if nxt is None:
            xp = _norm_res(y, norm_post_ffn[l], xp)
            xs = _norm_res(ys, norm_post_ffn[l], xs)
        else:
            xp, hp = _norm_res(y, norm_post_ffn[l], xp, nxt)
            xs, hs = _norm_res(ys, norm_post_ffn[l], xs, nxt)

    stack = lambda k: jnp.stack(outs[k])
    return (xp.reshape(bp, tp, d_model), xs.reshape(bs, ts, d_model),
            stack("p_cmp"), stack("p_sel"), stack("p_win"), stack("p_rec"),
            stack("s_cmp"), stack("s_sel"), stack("s_win"), stack("s_rec"))
```

```python
import functools
import math

import jax
import jax.numpy as jnp
from jax import lax
from jax.experimental import pallas as pl
from jax.experimental.pallas import tpu as pltpu

F32 = jnp.float32
BF16 = jnp.bfloat16

HGRN_DK = 128
NSA_HD = 128
NSA_GROUP = 4
BLOCK = 64
N_SELECT = 16
WINDOW = 512
PAGE_SIZE = 128
EPS = 1e-6
NEG_INF = -1e30
LB_FLOOR = 1e-30
ATTN_SCALE = NSA_HD ** -0.5
FORCE_SCORE = NSA_GROUP + 1.0

LANES = 128
VMEM_LIMIT = 56 * 1024 * 1024

HGRN_CHUNK = 64
HGRN_SUB = 16
HGRN_HEADS_PER_STEP = 4
PAGES_PER_STEP = 16
KEY_CHUNK = 512
NSA_QUERY_TILE = 256


def _cparams(sem):
    return pltpu.CompilerParams(dimension_semantics=sem, vmem_limit_bytes=VMEM_LIMIT)


def _dot(a, b):
    return jnp.dot(a, b, preferred_element_type=F32)


def _dot_nt(a, b):
    return lax.dot_general(a, b, (((1,), (1,)), ((), ())), preferred_element_type=F32)


def _dot_tn(a, b):
    return lax.dot_general(a, b, (((0,), (0,)), ((), ())), preferred_element_type=F32)


def _sigmoid(x):
    return 1.0 / (1.0 + jnp.exp(-x))


def _silu(x):
    return x * _sigmoid(x)


def _rms(x, g):
    ms = jnp.mean(x * x, axis=-1, keepdims=True)
    return x * lax.rsqrt(ms + EPS) * g


def _masked_softmax(s, mask):
    s = jnp.where(mask, s, NEG_INF)
    e = jnp.where(mask, jnp.exp(s - jnp.max(s, axis=-1, keepdims=True)), 0.0)
    return e / jnp.maximum(jnp.sum(e, axis=-1, keepdims=True), 1e-30)


def _rmsnorm_body(x_ref, g_ref, o_ref):
    o_ref[...] = _rms(x_ref[...], g_ref[...]).astype(o_ref.dtype)


def _rmsnorm_bf16(x, g):
    m, d = x.shape
    tm = min(m, 256)
    return pl.pallas_call(
        _rmsnorm_body,
        grid=(m // tm,),
        in_specs=[pl.BlockSpec((tm, d), lambda i: (i, 0)), pl.BlockSpec((1, d), lambda i: (0, 0))],
        out_specs=pl.BlockSpec((tm, d), lambda i: (i, 0)),
        out_shape=jax.ShapeDtypeStruct((m, d), BF16),
        compiler_params=_cparams(("parallel",)),
        name="rmsnorm",
    )(x, g.reshape(1, d))


def _norm_res_body(y_ref, g1_ref, r_ref, *rest):
    out = r_ref[...] + _rms(y_ref[...], g1_ref[...])
    if len(rest) == 1:
        rest[0][...] = out
    else:
        g2_ref, o_ref, n_ref = rest
        o_ref[...] = out
        n_ref[...] = _rms(out, g2_ref[...]).astype(n_ref.dtype)


def _norm_res(y, g1, res, g2=None):
    m, d = y.shape
    tm = min(m, 256)
    row = pl.BlockSpec((tm, d), lambda i: (i, 0))
    vec = pl.BlockSpec((1, d), lambda i: (0, 0))
    if g2 is None:
        return pl.pallas_call(
            _norm_res_body, grid=(m // tm,), in_specs=[row, vec, row], out_specs=row,
            out_shape=jax.ShapeDtypeStruct((m, d), F32),
            compiler_params=_cparams(("parallel",)), name="norm_res",
        )(y, g1.reshape(1, d), res)
    return pl.pallas_call(
        _norm_res_body, grid=(m // tm,), in_specs=[row, vec, row, vec], out_specs=[row, row],
        out_shape=[jax.ShapeDtypeStruct((m, d), F32), jax.ShapeDtypeStruct((m, d), BF16)],
        compiler_params=_cparams(("parallel",)), name="norm_res_norm",
    )(y, g1.reshape(1, d), res, g2.reshape(1, d))


def _matmul_body(x_ref, xs_ref, w_ref, o_ref, os_ref, *wb_ref, act, nk, w_t):
    i, kk = pl.program_id(1), pl.program_id(2)
    mm = _dot_nt if w_t else _dot

    def finish(part, ref, first):
        if nk == 1:
            if act:
                part = jnp.square(jnp.maximum(part, 0.0))
            ref[...] = part.astype(ref.dtype)
        else:
            @pl.when(first)
            def _():
                ref[...] = part

            @pl.when(jnp.logical_not(first))
            def _():
                ref[...] += part

    if nk == 1:
        @pl.when(i == 0)
        def _():
            wb_ref[0][...] = w_ref[...].astype(BF16)
        wb = wb_ref[0][...]
    else:
        wb = w_ref[...].astype(BF16)

    finish(mm(x_ref[...], wb), o_ref, kk == 0)

    @pl.when(i == 0)
    def _():
        finish(mm(xs_ref[...], wb), os_ref, kk == 0)


def _matmul(x, xs, w, layer, n_out, *, act=False, out_dtype=F32, w_t=False):
    m, k = x.shape
    ms = xs.shape[0]
    tm = min(m, 1024)
    tk, tn = (k, 512) if k <= 4096 else (2048, 1024)
    tn = math.gcd(tn, n_out)
    nk = k // tk
    assert m % tm == 0 and n_out % tn == 0 and k % tk == 0
    assert nk == 1 or (out_dtype == F32 and not act)
    if w_t:
        w_spec = pl.BlockSpec((None, tn, tk), lambda j, i, kk: (layer, j, kk))
        wb_shape = (tn, tk)
    else:
        w_spec = pl.BlockSpec((None, tk, tn), lambda j, i, kk: (layer, kk, j))
        wb_shape = (tk, tn)
    return pl.pallas_call(
        functools.partial(_matmul_body, act=act, nk=nk, w_t=w_t),
        grid=(n_out // tn, m // tm, nk),
        in_specs=[
            pl.BlockSpec((tm, tk), lambda j, i, kk: (i, kk)),
            pl.BlockSpec((ms, tk), lambda j, i, kk: (0, kk)),
            w_spec,
        ],
        out_specs=[
            pl.BlockSpec((tm, tn), lambda j, i, kk: (i, j)),
            pl.BlockSpec((ms, tn), lambda j, i, kk: (0, j)),
        ],
        out_shape=[
            jax.ShapeDtypeStruct((m, n_out), out_dtype),
            jax.ShapeDtypeStruct((ms, n_out), out_dtype),
        ],
        scratch_shapes=[pltpu.VMEM(wb_shape, BF16)] if nk == 1 else [],
        compiler_params=_cparams(("arbitrary", "arbitrary", "arbitrary")),
        name="matmul",
    )(x, xs, w)


def _hgrn_gates(z, lbp):
    a = lbp[0:1, :]
    log_sig = jnp.minimum(z, 0.0) - jnp.log1p(jnp.exp(-jnp.abs(z)))
    b = lbp[1:2, :] + log_sig
    logf = jnp.maximum(a, b) + jnp.log1p(jnp.exp(-jnp.abs(a - b)))
    k = lbp[2:3, :] * _sigmoid(-z)
    return logf, k


def _hgrn_out(o, gn, hg):
    return _rms(o, gn) * _silu(hg)


def _hgrn_chunk(q, z, v, hg, lbp, gn, st, consts):
    c, sub = HGRN_CHUNK, HGRN_SUB
    ns = c // sub
    tri, ones, t_in_sub = consts
    q = _silu(q)
    logf, k = _hgrn_gates(z, lbp)

    hi = logf.astype(BF16)
    r1 = logf - hi.astype(F32)
    mid = r1.astype(BF16)
    lo = (r1 - mid.astype(F32)).astype(BF16)
    cb = _dot(tri, hi) + _dot(tri, mid) + _dot(tri, lo)

    q3 = q.reshape(ns, sub, HGRN_DK)
    k3 = k.reshape(ns, sub, HGRN_DK)
    v3 = v.reshape(ns, sub, HGRN_DK)
    cb3 = cb.reshape(ns, sub, HGRN_DK)
    e_end = cb3[:, sub - 1:sub, :]
    b_beg = jnp.concatenate([jnp.zeros((1, 1, HGRN_DK), F32), e_end[:ns - 1]], axis=0)
    qt = q3 * jnp.exp(cb3 - b_beg)
    kt = k3 * jnp.exp(e_end - cb3)

    q_cols, k_cols = [], []
    zeros_sub = jnp.zeros((sub, HGRN_DK), F32)
    for j in range(ns - 1):
        qc, kc = [], []
        for i in range(ns):
            qc.append(qt[i] * jnp.exp(b_beg[i] - e_end[j]) if i > j else zeros_sub)
            kc.append(kt[i] if i == j else zeros_sub)
        q_cols.append(jnp.concatenate(qc, axis=0))
        k_cols.append(jnp.concatenate(kc, axis=0))
    q_hat = jnp.concatenate(q_cols, axis=1).astype(BF16)
    k_hat = jnp.concatenate(k_cols, axis=1).astype(BF16)
    att_off = _dot_nt(q_hat, k_hat)
    o = _dot(att_off.astype(BF16), v.astype(BF16))

    parts = []
    for s in range(sub):
        dec = jnp.exp(cb3 - cb3[:, s:s + 1, :])
        p = jnp.where(t_in_sub >= s, q3 * k3[:, s:s + 1, :] * dec, 0.0)
        parts.append(p.reshape(c, HGRN_DK).astype(BF16))
    rsum = _dot(jnp.concatenate(parts, axis=0), ones)
    o_diag = jnp.zeros((ns, sub, HGRN_DK), F32)
    for s in range(sub):
        o_diag = o_diag + rsum[s * c:(s + 1) * c, :].reshape(ns, sub, HGRN_DK) * v3[:, s:s + 1, :]
    o = o + o_diag.reshape(c, HGRN_DK)

    o = o + _dot_nt((q * jnp.exp(cb)).astype(BF16), st.astype(BF16))
    last = cb[c - 1:c, :]
    kd = (k * jnp.exp(last - cb)).astype(BF16)
    st_new = st * jnp.exp(last) + _dot_tn(v.astype(BF16), kd)
    return _hgrn_out(o, gn, hg), st_new


def _hgrn_prompt_body(q_ref, f_ref, i_ref, g_ref, lb_ref, gn_ref, o_ref, st_ref, s_ref, *, n_chunks):
    c, sub = HGRN_CHUNK, HGRN_SUB
    hp = s_ref.shape[0]
    tstep = pl.program_id(2)

    @pl.when(tstep == 0)
    def _():
        s_ref[...] = jnp.zeros_like(s_ref)

    row = lax.broadcasted_iota(jnp.int32, (c, c), 0)
    col = lax.broadcasted_iota(jnp.int32, (c, c), 1)
    consts = (
        jnp.where(row >= col, 1.0, 0.0).astype(BF16),
        jnp.ones((LANES, LANES), BF16),
        lax.broadcasted_iota(jnp.int32, (c // sub, sub, HGRN_DK), 1),
    )

    def chunk(ci, carry):
        rows = pl.ds(pl.multiple_of(ci * c, c), c)
        for hh in range(hp):
            cols = slice(hh * HGRN_DK, (hh + 1) * HGRN_DK)
            out, st_new = _hgrn_chunk(q_ref[rows, cols], f_ref[rows, cols], i_ref[rows, cols],
                                      g_ref[rows, cols], lb_ref[hh], gn_ref[hh], s_ref[hh], consts)
            s_ref[hh] = st_new
            o_ref[rows, cols] = out.astype(o_ref.dtype)
        return carry

    lax.fori_loop(0, n_chunks, chunk, 0, unroll=2)

    @pl.when(tstep == pl.num_programs(2) - 1)
    def _():
        for hh in range(hp):
            st_ref[hh] = s_ref[hh].T


def _hgrn_prompt(u, lbp, gn, *, batch, seq, heads, tt):
    nt = seq // tt
    hp = math.gcd(heads, HGRN_HEADS_PER_STEP)
    w = hp * HGRN_DK
    ng = heads // hp
    blk = lambda part: pl.BlockSpec((tt, w), lambda b, h, t: (b * nt + t, part * ng + h))
    return pl.pallas_call(
        functools.partial(_hgrn_prompt_body, n_chunks=tt // HGRN_CHUNK),
        grid=(batch, ng, nt),
        in_specs=[
            blk(0), blk(1), blk(2), blk(3),
            pl.BlockSpec((hp, 8, HGRN_DK), lambda b, h, t: (h, 0, 0)),
            pl.BlockSpec((hp, 1, HGRN_DK), lambda b, h, t: (h, 0, 0)),
        ],
        out_specs=[
            pl.BlockSpec((tt, w), lambda b, h, t: (b * nt + t, h)),
            pl.BlockSpec((None, hp, HGRN_DK, HGRN_DK), lambda b, h, t: (b, h, 0, 0)),
        ],
        out_shape=[
            jax.ShapeDtypeStruct((batch * seq, heads * HGRN_DK), BF16),
            jax.ShapeDtypeStruct((batch, heads, HGRN_DK, HGRN_DK), F32),
        ],
        scratch_shapes=[pltpu.VMEM((hp, HGRN_DK, HGRN_DK), F32)],
        compiler_params=_cparams(("parallel", "parallel", "arbitrary")),
        name="hgrn_prompt",
    )(u, u, u, u, lbp, gn)


def _hgrn_decode_body(q_ref, f_ref, i_ref, g_ref, lb_ref, gn_ref, s0_ref, o_ref, st_ref):
    logf, k = _hgrn_gates(f_ref[...], lb_ref[...])
    decay = jnp.exp(logf)
    q = _silu(q_ref[...])
    v = i_ref[...]

    def col(x):
        return jnp.broadcast_to(x, (HGRN_DK, HGRN_DK)).T

    outs = []
    for b in range(s0_ref.shape[0]):
        r = slice(b, b + 1)
        s_new = col(decay[r]) * s0_ref[b] + col(k[r]) * v[r]
        st_ref[b] = s_new
        outs.append(jnp.sum(col(q[r]) * s_new, axis=0, keepdims=True))
    o_ref[...] = _hgrn_out(jnp.concatenate(outs, axis=0), gn_ref[...], g_ref[...])


def _hgrn_decode(u, lbp, gn, s0, *, layer, batch, heads):
    blk = lambda off: pl.BlockSpec((batch, HGRN_DK), lambda h: (0, off + h))
    return pl.pallas_call(
        _hgrn_decode_body,
        grid=(heads,),
        in_specs=[
            blk(0), blk(heads), blk(2 * heads), blk(3 * heads),
            pl.BlockSpec((None, 8, HGRN_DK), lambda h: (h, 0, 0)),
            pl.BlockSpec((None, 1, HGRN_DK), lambda h: (h, 0, 0)),
            pl.BlockSpec((None, batch, None, HGRN_DK, HGRN_DK), lambda h: (layer, 0, h, 0, 0)),
        ],
        out_specs=[
            pl.BlockSpec((batch, HGRN_DK), lambda h: (0, h)),
            pl.BlockSpec((batch, None, HGRN_DK, HGRN_DK), lambda h: (0, h, 0, 0)),
        ],
        out_shape=[
            jax.ShapeDtypeStruct((batch, heads * HGRN_DK), F32),
            jax.ShapeDtypeStruct(s0.shape[1:], F32),
        ],
        compiler_params=_cparams(("parallel",)),
        name="hgrn_decode",
    )(u, u, u, u, lbp, gn, s0)


def _kv_rows_body(*refs, n, hc):
    for x_ref, o_ref in zip(refs[:n], refs[-n:]):
        tm = x_ref.shape[0]
        for j in range(hc):
            o_ref[pl.ds(j, tm, stride=hc), :] = x_ref[:, j * NSA_HD:(j + 1) * NSA_HD]


def _kv_rows(u, cols, kv_w, *, layer, depth, stacked=None):
    m = u.shape[0]
    tm = min(m, 256)
    nt = m // tm
    hc = kv_w // NSA_HD
    n = len(cols)
    assert all(c % kv_w == 0 for c in cols)
    in_specs = [pl.BlockSpec((tm, kv_w), lambda i, c=c: (i, c // kv_w)) for c in cols]
    args = [u] * n
    aliases = {}
    if stacked is not None:
        in_specs += [pl.BlockSpec(memory_space=pl.ANY)] * n
        args += list(stacked)
        aliases = {n + k: k for k in range(n)}
    return pl.pallas_call(
        functools.partial(_kv_rows_body, n=n, hc=hc),
        grid=(nt,),
        in_specs=in_specs,
        out_specs=[pl.BlockSpec((tm * hc, NSA_HD), lambda i: (layer * nt + i, 0)) for _ in cols],
        out_shape=[jax.ShapeDtypeStruct((depth * m * hc, NSA_HD), F32) for _ in cols],
        input_output_aliases=aliases,
        compiler_params=_cparams(("parallel",)),
        name="kv_rows",
    )(*args)


def _compress_paged_body(pt_ref, *refs, kvh, npg):
    del pt_ref
    page_refs = refs[:npg]
    pe_ref, w_ref, o_ref, lhs_ref = refs[npg:]
    hc = 2 * kvh
    bpp = PAGE_SIZE // BLOCK
    nblk = npg * bpp
    for r in range(BLOCK):
        rows = [page_refs[p][(half * BLOCK + r) * hc:(half * BLOCK + r + 1) * hc, :]
                for p in range(npg) for half in range(bpp)]
        x = jnp.concatenate(rows, axis=0).reshape(nblk, hc, NSA_HD) + pe_ref[r]
        lhs_ref[:, r * NSA_HD:(r + 1) * NSA_HD] = x.reshape(nblk * hc, NSA_HD).astype(BF16)
    y = _dot(lhs_ref[...], w_ref[...])
    is_v = lax.broadcasted_iota(jnp.int32, (nblk * hc, NSA_HD), 0) % 2 == 1
    out = jnp.where(is_v, y[:, NSA_HD:], y[:, :NSA_HD])
    o_ref[...] = out.reshape(nblk, hc, NSA_HD)


def _compress_paged(pool, page_table, pe, w, *, layer, kvh):
    batch, n_pages = page_table.shape
    npg = math.gcd(n_pages, PAGES_PER_STEP)
    rpp = pool.shape[2]
    hc = 2 * kvh
    nblk = npg * PAGE_SIZE // BLOCK

    def page_spec(p):
        return pl.BlockSpec((None, None, rpp, NSA_HD),
                            lambda b, g, pt: (layer, pt[b, g * npg + p], 0, 0))

    grid_spec = pltpu.PrefetchScalarGridSpec(
        num_scalar_prefetch=1,
        grid=(batch, n_pages // npg),
        in_specs=[page_spec(p) for p in range(npg)] + [
            pl.BlockSpec((BLOCK, hc, NSA_HD), lambda b, g, pt: (0, 0, 0)),
            pl.BlockSpec((BLOCK * NSA_HD, 2 * NSA_HD), lambda b, g, pt: (0, 0)),
        ],
        out_specs=pl.BlockSpec((None, nblk, hc, NSA_HD), lambda b, g, pt: (b, g, 0, 0)),
        scratch_shapes=[pltpu.VMEM((nblk * hc, BLOCK * NSA_HD), BF16)],
    )
    return pl.pallas_call(
        functools.partial(_compress_paged_body, kvh=kvh, npg=npg),
        grid_spec=grid_spec,
        out_shape=jax.ShapeDtypeStruct((batch, n_pages * PAGE_SIZE // BLOCK, hc, NSA_HD), F32),
        compiler_params=_cparams(("parallel", "arbitrary")),
        name="compress_paged",
    )(page_table, *([pool] * npg), pe, w)


def _nsa_prompt_body(q_ref, kc_ref, vct_ref, skv_ref, wkv_ref, gate_ref, e_ref, o_ref,
                     ks_ref, vst_ref, kw_ref, vwt_ref, m_ref, l_ref, acc_ref, *, tq, seq, n_blk, span, kc_len):
    i = pl.program_id(2)

    @pl.when(i == 0)
    def _():
        ks_ref[...] = skv_ref[:, :NSA_HD].astype(BF16)
        kw_ref[...] = wkv_ref[:, :NSA_HD].astype(BF16)
        for c in range(seq // kc_len):
            vst_ref[c] = skv_ref[c * kc_len:(c + 1) * kc_len, NSA_HD:].T.astype(BF16)
        for c in range(seq // tq):
            vwt_ref[c] = wkv_ref[c * tq:(c + 1) * tq, NSA_HD:].T.astype(BF16)

    nbp = kc_ref.shape[0]
    nb8 = -(-n_blk // 8) * 8
    n_sel = min(N_SELECT, n_blk)
    q0 = i * tq
    qpos = q0 + lax.broadcasted_iota(jnp.int32, (1, tq), 1)
    cur = qpos // BLOCK
    n_sub = lax.broadcasted_iota(jnp.int32, (nbp, 1), 0)
    cmask = (n_sub * BLOCK + (BLOCK - 1) <= qpos) & (n_sub < seq // BLOCK)

    kc = kc_ref[...]
    vct = vct_ref[...]
    q_all = jnp.concatenate(
        [(q_ref[:, g * NSA_HD:(g + 1) * NSA_HD] * ATTN_SCALE).astype(BF16) for g in range(NSA_GROUP)], axis=0)
    per_head = lambda a: jnp.concatenate([a] * NSA_GROUP, axis=1)
    head = lambda a, g: a[:, g * tq:(g + 1) * tq]

    cmask4 = per_head(cmask)
    s = jnp.where(cmask4, _dot_nt(kc, q_all), NEG_INF)
    e = jnp.where(cmask4, jnp.exp(s - jnp.max(s, axis=0, keepdims=True)), 0.0)
    p = e / jnp.maximum(jnp.sum(e, axis=0, keepdims=True), 1e-30)
    o_c = _dot(vct, p.astype(BF16))
    imp = head(p, 0)
    for g in range(1, NSA_GROUP):
        imp = imp + head(p, g)

    forced = (n_sub == 0) | (n_sub == cur) | (n_sub == cur - 1)
    score = jnp.where(n_sub <= cur, jnp.where(forced, FORCE_SCORE, imp), -1.0)[:nb8, :]
    j_idx = lax.broadcasted_iota(jnp.int32, (nb8, tq), 0)
    score = jnp.where(j_idx < n_blk, score, -2.0)
    cnt = jnp.zeros((nb8, tq), F32)
    for j in range(n_blk):
        rj = score[j:j + 1, :]
        beats = jnp.where(rj > score, 1.0, jnp.where((rj == score) & (j_idx > j), 1.0, 0.0))
        cnt = cnt + beats
    sel = jnp.where((cnt < n_sel) & (j_idx <= cur) & (j_idx < n_blk), 1.0, 0.0)
    if nbp > nb8:
        sel = jnp.concatenate([sel, jnp.zeros((nbp - nb8, tq), F32)], axis=0)
    sel = sel.astype(BF16)

    m_ref[...] = jnp.full_like(m_ref, NEG_INF)
    l_ref[...] = jnp.zeros_like(l_ref)
    acc_ref[...] = jnp.zeros_like(acc_ref)

    def key_chunk(ci, carry):
        k0 = pl.multiple_of(ci * kc_len, kc_len)
        kk = ks_ref[pl.ds(k0, kc_len), :]
        vt = vst_ref[ci]
        kpos = k0 + lax.broadcasted_iota(jnp.int32, (kc_len, 1), 0)
        bias = jnp.where((_dot(e_ref[ci], sel) > 0.5) & (kpos <= qpos), 0.0, NEG_INF)
        s = _dot_nt(kk, q_all) + per_head(bias)
        m_old = m_ref[...]
        m_new = jnp.maximum(m_old, jnp.max(s, axis=0, keepdims=True))
        alpha = jnp.exp(m_old - m_new)
        p = jnp.exp(s - m_new)
        l_ref[...] = alpha * l_ref[...] + jnp.sum(p, axis=0, keepdims=True)
        acc_ref[...] = alpha * acc_ref[...] + _dot(vt, p.astype(BF16))
        m_ref[...] = m_new
        return carry

    lax.fori_loop(0, (q0 + tq + kc_len - 1) // kc_len, key_chunk, 0)

    nwb = span // tq
    blk0 = jnp.maximum(i + 1 - nwb, 0)
    start = pl.multiple_of(blk0 * tq, tq)
    wpos = start + lax.broadcasted_iota(jnp.int32, (span, 1), 0)
    dist = qpos - wpos
    wbias = jnp.where((dist >= 0) & (dist < WINDOW), 0.0, NEG_INF)
    kw = kw_ref[pl.ds(start, span), :]
    vwt = jnp.concatenate([vwt_ref[blk0 + j] for j in range(nwb)], axis=1)

    o_s = acc_ref[...] / l_ref[...]
    s = _dot_nt(kw, q_all) + per_head(wbias)
    e = jnp.exp(s - jnp.max(s, axis=0, keepdims=True))
    o_w = _dot(vwt, e.astype(BF16)) / jnp.sum(e, axis=0, keepdims=True)

    gates = _sigmoid(gate_ref[...]).T
    for g in range(NSA_GROUP):
        gc = gates[g:g + 1, :]
        gs = gates[NSA_GROUP + g:NSA_GROUP + g + 1, :]
        gw = gates[2 * NSA_GROUP + g:2 * NSA_GROUP + g + 1, :]
        o = gc * head(o_c, g) + gs * head(o_s, g) + gw * head(o_w, g)
        o_ref[:, g * NSA_HD:(g + 1) * NSA_HD] = o.T.astype(o_ref.dtype)


def _nsa_prompt(u, ug, kc, vct, expand, *, batch, seq, kvh, col_q, col_skv, col_wkv):
    tq = min(seq, NSA_QUERY_TILE)
    assert tq % LANES == 0 and seq % tq == 0
    nq = seq // tq
    span = min(seq, WINDOW + tq)
    qw = NSA_GROUP * NSA_HD
    nbp = kc.shape[2]
    n_kc, kc_len, _ = expand.shape
    return pl.pallas_call(
        functools.partial(_nsa_prompt_body, tq=tq, seq=seq, n_blk=seq // BLOCK, span=span, kc_len=kc_len),
        grid=(batch, kvh, nq),
        in_specs=[
            pl.BlockSpec((tq, qw), lambda b, h, i: (b * nq + i, col_q // qw + h)),
            pl.BlockSpec((None, None, nbp, NSA_HD), lambda b, h, i: (b, h, 0, 0)),
            pl.BlockSpec((None, None, NSA_HD, nbp), lambda b, h, i: (b, h, 0, 0)),
            pl.BlockSpec((seq, 2 * NSA_HD), lambda b, h, i: (b, col_skv // (2 * NSA_HD) + h)),
            pl.BlockSpec((seq, 2 * NSA_HD), lambda b, h, i: (b, col_wkv // (2 * NSA_HD) + h)),
            pl.BlockSpec((tq, LANES), lambda b, h, i: (b * nq + i, h)),
            pl.BlockSpec((n_kc, kc_len, nbp), lambda b, h, i: (0, 0, 0)),
        ],
        out_specs=pl.BlockSpec((tq, qw), lambda b, h, i: (b * nq + i, h)),
        out_shape=jax.ShapeDtypeStruct((batch * seq, kvh * qw), BF16),
        scratch_shapes=[
            pltpu.VMEM((seq, NSA_HD), BF16),
            pltpu.VMEM((n_kc, NSA_HD, kc_len), BF16),
            pltpu.VMEM((seq, NSA_HD), BF16),
            pltpu.VMEM((nq, NSA_HD, tq), BF16),
            pltpu.VMEM((1, NSA_GROUP * tq), F32),
            pltpu.VMEM((1, NSA_GROUP * tq), F32),
            pltpu.VMEM((NSA_HD, NSA_GROUP * tq), F32),
        ],
        compiler_params=_cparams(("parallel", "parallel", "arbitrary")),
        name="nsa_prompt",
    )(u, kc, vct, u, u, ug, expand)


def _nsa_decode_select_body(q_ref, kc_ref, oc_ref, idx_ref, *, past_len, n_blk):
    nbc = kc_ref.shape[1]
    npad = -(-n_blk // LANES) * LANES
    n_sel = min(N_SELECT, n_blk)
    qpos = past_len
    cur = qpos // BLOCK
    q = q_ref[...].astype(BF16)
    n_lane = lax.broadcasted_iota(jnp.int32, (1, nbc), 1)
    cmask = n_lane * BLOCK + (BLOCK - 1) <= qpos
    p = _masked_softmax(_dot_nt(q, kc_ref[0].astype(BF16)) * ATTN_SCALE, cmask)
    oc_ref[...] = _dot(p.astype(BF16), kc_ref[1].astype(BF16))
    imp = jnp.sum(p, axis=0, keepdims=True)
    imp = jnp.concatenate([imp, jnp.zeros((1, npad - nbc), F32)], axis=1)

    j_lane = lax.broadcasted_iota(jnp.int32, (1, npad), 1)
    forced = (j_lane == 0) | (j_lane == cur) | (j_lane == cur - 1)
    score = jnp.where(j_lane <= cur, jnp.where(forced, FORCE_SCORE, imp), -1.0)
    score = jnp.where(j_lane < n_blk, score, -2.0)
    s_row = jnp.broadcast_to(score, (npad, npad))
    s_col = s_row.T
    jp = lax.broadcasted_iota(jnp.int32, (npad, npad), 0)
    jj = lax.broadcasted_iota(jnp.int32, (npad, npad), 1)
    beats = jnp.where(s_col > s_row, 1.0, jnp.where((s_col == s_row) & (jp < jj), 1.0, 0.0))
    rank = jnp.sum(beats, axis=0, keepdims=True)
    k_sub = lax.broadcasted_iota(jnp.int32, (N_SELECT, npad), 0).astype(F32)
    j_f = lax.broadcasted_iota(jnp.int32, (N_SELECT, npad), 1).astype(F32)
    hit = jnp.where((rank == k_sub) & (j_f < n_blk), j_f, 0.0)
    idx = jnp.sum(hit, axis=1, keepdims=True)
    idx = jnp.where(k_sub[:, :1] < n_sel, idx, 0.0)
    idx_ref[...] = jnp.broadcast_to(idx, (N_SELECT, LANES)).astype(jnp.int32)


def _nsa_decode_select(q, kc, *, past_len, n_blk):
    batch, kvh = q.shape[:2]
    nbc = kc.shape[3]
    return pl.pallas_call(
        functools.partial(_nsa_decode_select_body, past_len=past_len, n_blk=n_blk),
        grid=(batch, kvh),
        in_specs=[
            pl.BlockSpec((None, None, NSA_GROUP, NSA_HD), lambda b, h: (b, h, 0, 0)),
            pl.BlockSpec((2, None, None, nbc, NSA_HD), lambda b, h: (0, b, h, 0, 0)),
        ],
        out_specs=[
            pl.BlockSpec((None, None, NSA_GROUP, NSA_HD), lambda b, h: (b, h, 0, 0)),
            pl.BlockSpec((None, None, N_SELECT, LANES), lambda b, h: (b, h, 0, 0)),
        ],
        out_shape=[
            jax.ShapeDtypeStruct((batch, kvh, NSA_GROUP, NSA_HD), F32),
            jax.ShapeDtypeStruct((batch, kvh, N_SELECT, LANES), jnp.int32),
        ],
        compiler_params=_cparams(("parallel", "parallel")),
        name="nsa_decode_select",
    )(q, kc)


def _nsa_decode_attend_body(phys_ref, isnew_ref, q_ref, *refs, n_sel, kvh):
    b, h = pl.program_id(0), pl.program_id(1)
    del phys_ref
    blk_refs = refs[:n_sel]
    snew_ref, win_ref, wnew_ref, gate_ref, oc_ref, o_ref = refs[n_sel:]
    q = q_ref[...]
    qb = q.astype(BF16)
    rstride = 2 * kvh

    def attend(kb, vb, valid, k_new, v_new, use_new):
        s = jnp.where(valid, _dot_nt(qb, kb) * ATTN_SCALE, NEG_INF)
        s_n = jnp.where(use_new, jnp.sum(q * k_new, axis=-1, keepdims=True) * ATTN_SCALE, NEG_INF)
        mx = jnp.maximum(jnp.max(s, axis=-1, keepdims=True), s_n)
        e = jnp.where(valid, jnp.exp(s - mx), 0.0)
        e_n = jnp.where(use_new, jnp.exp(s_n - mx), 0.0)
        den = jnp.maximum(jnp.sum(e, axis=-1, keepdims=True) + e_n, 1e-30)
        return (_dot(e.astype(BF16), vb) + e_n * v_new) / den

    lane_blk = lax.broadcasted_iota(jnp.int32, (1, n_sel * BLOCK), 1) // BLOCK
    valid = jnp.zeros((1, n_sel * BLOCK), jnp.int32)
    any_new = jnp.int32(0)
    for k in range(n_sel):
        is_new = isnew_ref[b, h, k]
        valid = jnp.where(lane_blk == k, 1 - is_new, valid)
        any_new = jnp.maximum(any_new, is_new)
    kb = jnp.concatenate([r[pl.ds(2 * h, BLOCK, stride=rstride), :] for r in blk_refs], axis=0)
    vb = jnp.concatenate([r[pl.ds(2 * h + 1, BLOCK, stride=rstride), :] for r in blk_refs], axis=0)
    o_s = attend(kb.astype(BF16), vb.astype(BF16), valid > 0,
                 snew_ref[:, :NSA_HD], snew_ref[:, NSA_HD:], any_new > 0)

    wb = win_ref.shape[0] // rstride
    kw = win_ref[pl.ds(2 * h, wb, stride=rstride), :].astype(BF16)
    vw = win_ref[pl.ds(2 * h + 1, wb, stride=rstride), :].astype(BF16)
    prow = lax.broadcasted_iota(jnp.int32, (1, wb), 1)
    o_w = attend(kw, vw, prow >= 1, wnew_ref[:, :NSA_HD], wnew_ref[:, NSA_HD:], True)

    gates = _sigmoid(gate_ref[...])
    o_ref[...] = gates[0] * oc_ref[...] + gates[1] * o_s + gates[2] * o_w


def _nsa_decode_attend(phys, isnew, q, pool, snew, win, wnew, gate, o_c, *, layer):
    batch, kvh, n_sel = phys.shape
    kv2 = 2 * NSA_HD
    brows = pool.shape[2]
    wrows = win.shape[2]
    def blk_spec(k):
        return pl.BlockSpec((None, None, brows, NSA_HD), lambda b, h, ph, nw: (layer, ph[b, h, k], 0, 0))

    head = pl.BlockSpec((None, None, NSA_GROUP, NSA_HD), lambda b, h, ph, nw: (b, h, 0, 0))
    new_row = pl.BlockSpec((None, None, 1, kv2), lambda b, h, ph, nw: (b, h, 0, 0))
    grid_spec = pltpu.PrefetchScalarGridSpec(
        num_scalar_prefetch=2,
        grid=(batch, kvh),
        in_specs=[head] + [blk_spec(k) for k in range(n_sel)] + [
            new_row,
            pl.BlockSpec((None, None, wrows, NSA_HD), lambda b, h, ph, nw: (layer, b, 0, 0)),
            new_row,
            pl.BlockSpec((None, None, 3, NSA_GROUP, NSA_HD), lambda b, h, ph, nw: (b, h, 0, 0, 0)),
            head,
        ],
        out_specs=head,
    )
    return pl.pallas_call(
        functools.partial(_nsa_decode_attend_body, n_sel=n_sel, kvh=kvh),
        grid_spec=grid_spec,
        out_shape=jax.ShapeDtypeStruct((batch, kvh, NSA_GROUP, NSA_HD), F32),
        compiler_params=_cparams(("parallel", "arbitrary")),
        name="nsa_decode_attend",
    )(phys, isnew, q, *([pool] * n_sel), snew, win, wnew, gate, o_c)


def kernel(x_prompt, x_sample, cache_cmp_kv, cache_sel_kv, state_win_kv, state_hgrn, page_table,
           w_in, w_out, w_up, w_down, cmp_pe, cmp_w, lb_logits, hgrn_norm,
           norm_pre_mix, norm_post_mix, norm_pre_ffn, norm_post_ffn):
    depth = w_in.shape[0]
    bp, tp, d_model = x_prompt.shape
    bs, ts, _ = x_sample.shape
    assert ts == 1, "decode path handles one new token per sequence"
    d_hgrn = hgrn_norm.shape[1]
    heads = d_hgrn // HGRN_DK
    d_nsa = w_out.shape[1] - d_hgrn
    d_ff = w_up.shape[2]
    kvh = d_nsa // NSA_HD // NSA_GROUP
    kv_w = kvh * 2 * NSA_HD
    n_main = 4 * d_hgrn + d_nsa + 3 * kv_w
    assert w_in.shape[2] == n_main + 3 * kvh * NSA_GROUP
    col_q = 4 * d_hgrn
    col_ckv = col_q + d_nsa
    col_skv = col_ckv + kv_w
    col_wkv = col_skv + kv_w
    past_len = page_table.shape[1] * PAGE_SIZE
    wb = state_win_kv.shape[2]
    assert wb == WINDOW and past_len % BLOCK == 0
    nb_p = tp // BLOCK
    nb_past = past_len // BLOCK
    n_blk_s = nb_past + 1

    p_lb = jax.nn.softmax(lb_logits.astype(F32), axis=0)
    lbs = jnp.cumsum(p_lb, axis=0) - p_lb[0:1]
    w_in_t = jnp.swapaxes(w_in, 1, 2)

    def gate_weights(l):
        gate = w_in_t[l, n_main:].reshape(3, kvh, NSA_GROUP, d_model).transpose(1, 0, 2, 3)
        gate = gate.reshape(kvh, 3 * NSA_GROUP, d_model)
        return jnp.pad(gate, ((0, 0), (0, LANES - 3 * NSA_GROUP), (0, 0))).reshape(1, kvh * LANES, d_model)

    def lb_rows(lb):
        lb = lb.reshape(heads, 1, HGRN_DK)
        rows = jnp.concatenate([jnp.log(jnp.maximum(lb, LB_FLOOR)), jnp.log1p(-lb), 1.0 - lb], axis=1)
        return jnp.pad(rows, ((0, 0), (0, 5), (0, 0)))

    kc_len = min(tp, KEY_CHUNK)
    nbp = -(-nb_p // LANES) * LANES
    key_blk = (jnp.arange(tp) // BLOCK).reshape(tp // kc_len, kc_len, 1)
    expand = (jnp.arange(nbp)[None, None, :] == key_blk).astype(BF16)
    pool_cmp = cache_cmp_kv.reshape(depth, cache_cmp_kv.shape[1], PAGE_SIZE * kvh * 2, NSA_HD)
    pool_sel = cache_sel_kv.reshape(depth, cache_sel_kv.shape[1] * (PAGE_SIZE // BLOCK), BLOCK * kvh * 2, NSA_HD)
    win_state = state_win_kv.reshape(depth, bs, wb * kvh * 2, NSA_HD)
    assert tp % PAGE_SIZE == 0
    prompt_pages = jnp.arange(bp * tp // PAGE_SIZE, dtype=jnp.int32).reshape(bp, tp // PAGE_SIZE)

    xp = x_prompt.reshape(bp * tp, d_model)
    xs = x_sample.reshape(bs, d_model)
    hp = _rmsnorm_bf16(xp, norm_pre_mix[0])
    hs = _rmsnorm_bf16(xs, norm_pre_mix[0])
    outs = {k: [] for k in ("p_rec", "s_cmp", "s_sel", "s_win", "s_rec")}
    kv5 = lambda a, b, t: a.reshape(b, t, kvh, 2, NSA_HD)
    p_kv = None

    for l in range(depth):
        w_gate = gate_weights(l)
        w_c = cmp_w[l].reshape(2, BLOCK * NSA_HD, NSA_HD)
        w_c = jnp.concatenate([w_c[0], w_c[1]], axis=1).astype(BF16)
        pe_c = jnp.tile(cmp_pe[l], (1, kvh, 1))
        lbp = lb_rows(lbs[l])
        gn = hgrn_norm[l].reshape(heads, 1, HGRN_DK)
        nxt = norm_pre_mix[l + 1] if l + 1 < depth else None

        u, us = _matmul(hp, hs, w_in_t, l, n_main, w_t=True)
        ug, ugs = _matmul(hp, hs, w_gate, 0, kvh * LANES, w_t=True)

        o_r, st = _hgrn_prompt(u, lbp, gn, batch=bp, seq=tp, heads=heads, tt=min(tp, 256))
        p_kv = _kv_rows(u, (col_ckv, col_skv, col_wkv), kv_w, layer=l, depth=depth, stacked=p_kv)
        kc = _compress_paged(p_kv[0].reshape(depth, bp * tp // PAGE_SIZE, PAGE_SIZE * kvh * 2, NSA_HD),
                             prompt_pages, pe_c, w_c, layer=l, kvh=kvh)
        kc = kc.reshape(bp, nb_p, kvh, 2, NSA_HD).transpose(3, 0, 2, 1, 4)
        kc = jnp.pad(kc, ((0, 0), (0, 0), (0, 0), (0, nbp - nb_p), (0, 0)))
        o_b = _nsa_prompt(u, ug, kc[0].astype(BF16), jnp.swapaxes(kc[1], 2, 3).astype(BF16), expand,
                          batch=bp, seq=tp, kvh=kvh, col_q=col_q, col_skv=col_skv, col_wkv=col_wkv)
        mix_p = jnp.concatenate([o_r, o_b], axis=1)
        outs["p_rec"].append(st)

        o_r, st = _hgrn_decode(us, lbp, gn, state_hgrn, layer=l, batch=bs, heads=heads)
        q_s = us[:, col_q:col_ckv].reshape(bs, kvh, NSA_GROUP, NSA_HD)
        ckv = us[:, col_ckv:col_skv]
        skv = us[:, col_skv:col_wkv]
        wkv = us[:, col_wkv:n_main]
        kc_s = _compress_paged(pool_cmp, page_table, pe_c, w_c, layer=l, kvh=kvh)
        kc_s = kc_s.reshape(bs, nb_past, kvh, 2, NSA_HD).transpose(3, 0, 2, 1, 4)
        o_c, idx = _nsa_decode_select(q_s, kc_s, past_len=past_len, n_blk=n_blk_s)
        idx = idx[..., 0]
        ppos = jnp.minimum(idx, nb_past - 1) * BLOCK
        page = jnp.take_along_axis(page_table, (ppos // PAGE_SIZE).reshape(bs, -1), axis=1)
        phys = page.reshape(idx.shape) * (PAGE_SIZE // BLOCK) + (ppos % PAGE_SIZE) // BLOCK
        isnew = (idx >= nb_past).astype(jnp.int32)
        gate_s = ugs.reshape(bs, kvh, LANES)[:, :, :3 * NSA_GROUP]
        gate_s = jnp.broadcast_to(gate_s.reshape(bs, kvh, 3, NSA_GROUP, 1), (bs, kvh, 3, NSA_GROUP, NSA_HD))
        o_b = _nsa_decode_attend(phys.astype(jnp.int32), isnew, q_s, pool_sel,
                                 skv.reshape(bs, kvh, 1, 2 * NSA_HD), win_state,
                                 wkv.reshape(bs, kvh, 1, 2 * NSA_HD), gate_s, o_c, layer=l)
        mix_s = jnp.concatenate([o_r, o_b.reshape(bs, d_nsa)], axis=1).astype(BF16)
        outs["s_cmp"].append(kv5(ckv, bs, ts))
        outs["s_sel"].append(kv5(skv, bs, ts))
        outs["s_win"].append(jnp.concatenate([state_win_kv[l][:, ts:], kv5(wkv, bs, ts)], axis=1))
        outs["s_rec"].append(st)

        y, ys = _matmul(mix_p, mix_s, w_out, l, d_model)
        xp, fp = _norm_res(y, norm_post_mix[l], xp, norm_pre_ffn[l])
        xs, fs = _norm_res(ys, norm_post_mix[l], xs, norm_pre_ffn[l])
        fp, fs = _matmul(fp, fs, w_up, l, d_ff, act=True, out_dtype=BF16)
        y, ys = _matmul(fp, fs, w_down, l, d_model)
        if nxt is None:
            xp = _norm_res(y, norm_post_ffn[l], xp)
            xs = _norm_res(ys, norm_post_ffn[l], xs)
        else:
            xp, hp = _norm_res(y, norm_post_ffn[l], xp, nxt)
            xs, hs = _norm_res(ys, norm_post_ffn[l], xs, nxt)

    stack = lambda k: jnp.stack(outs[k])
    p_cmp, p_sel, p_win = (a.reshape(depth, bp, tp, kvh, 2, NSA_HD) for a in p_kv)
    return (xp.reshape(bp, tp, d_model), xs.reshape(bs, ts, d_model),
            p_cmp, p_sel, p_win[:, :, tp - min(WINDOW, tp):], stack("p_rec"),
            stack("s_cmp"), stack("s_sel"), stack("s_win"), stack("s_rec"))
```
